```python
import math
import functools
import jax
import jax.numpy as jnp
from jax import lax
import numpy as np

D_MODEL = 2048
BATCH = 4
SEQ = 2048
DEPTH = 4
DEC_BATCH = 8
DEC_SEQ = 8
PAST_LEN = 16384
PAGE_SIZE = 128

M_HEADS = 4
M_DK = 256
M_DV = 256
M_WIDTH = M_HEADS * M_DV
QK_WIDTH = 2 * M_HEADS * M_DK
CONV_W = 4
CHUNK = 128
A_HEADS = 8
A_DH = 128
A_WIDTH = A_HEADS * A_DH
PATTERNS = ((128, 1), (512, 4), (2048, 16))
MAX_WINDOW = 2048
Q_BLOCK = 128
N_BUCKETS = 32
MAX_DISTANCE = MAX_WINDOW
D_FF = 5632
EPS = 1e-6
IN_SIZES = (QK_WIDTH, M_WIDTH, M_WIDTH, 2 * M_HEADS, A_WIDTH, A_WIDTH, A_WIDTH, D_MODEL, D_MODEL)
N_IN = QK_WIDTH + 2 * M_WIDTH + 2 * M_HEADS + 3 * A_WIDTH + 2 * D_MODEL

kernel_name = 'dilated_mlstm_macaron_hybrid'


def rmsnorm(x, g):
    x32 = x.astype(jnp.float32)
    y = x32 * lax.rsqrt(jnp.mean(x32 * x32, axis=-1, keepdims=True) + EPS)
    return (y * g.astype(jnp.float32)).astype(x.dtype)


def swiglu(x, w_in, w_out):
    gate, up = jnp.split(x @ w_in, 2, axis=-1)
    return (jax.nn.silu(gate) * up) @ w_out


def split_cols(proj):
    bounds = np.cumsum(IN_SIZES)[:-1].tolist()
    return jnp.split(proj, bounds, axis=-1)


def t5_bucket(dist):
    exact = N_BUCKETS // 2
    d32 = jnp.maximum(dist, 1).astype(jnp.float32)
    large = exact + (jnp.log(d32 / exact) / math.log(MAX_DISTANCE / exact) * (N_BUCKETS - exact)).astype(jnp.int32)
    large = jnp.minimum(large, N_BUCKETS - 1)
    return jnp.where(dist < exact, dist, large)


def pattern_biases(rel_table):
    tab = rel_table.astype(jnp.float32)
    return [tab[t5_bucket(d * jnp.arange(w // d + 1, dtype=jnp.int32))] for (w, d) in PATTERNS]


def combine_groups(parts):
    M = parts[0][2]
    for _, _, m in parts[1:]:
        M = jnp.maximum(M, m)
    wts = [den * jnp.exp(m - M) for _, den, m in parts]
    outs = [num / den[..., None] for num, den, _ in parts]
    total = wts[0][..., None] * outs[0]
    wsum = wts[0]
    for w, o in zip(wts[1:], outs[1:]):
        total = total + w[..., None] * o
        wsum = wsum + w
    return total / wsum[..., None]


def dilated_attention_prompt(q, k, v, biases):
    B, S, H, E = q.shape
    f32 = jnp.float32
    pad = ((0, 0), (MAX_WINDOW, 0), (0, 0), (0, 0))
    kp = jnp.pad(k, pad)
    vp = jnp.pad(v, pad)

    def block(bi):
        b0 = bi * Q_BLOCK
        qb = lax.dynamic_slice_in_dim(q, b0, Q_BLOCK, axis=1)
        parts = []
        for (w, d), bias in zip(PATTERNS, biases):
            nb = w // d
            L = w + Q_BLOCK
            ks = lax.dynamic_slice_in_dim(kp, b0 + MAX_WINDOW - w, L, axis=1)
            vs = lax.dynamic_slice_in_dim(vp, b0 + MAX_WINDOW - w, L, axis=1)
            qr = qb.reshape(B, Q_BLOCK // d, d, H, E)
            kr = ks.reshape(B, L // d, d, H, E)
            vr = vs.reshape(B, L // d, d, H, E)
            s = jnp.einsum('bcrhe,barhe->bhrca', qr, kr).astype(f32)
            c = jnp.arange(Q_BLOCK // d)
            a = jnp.arange(L // d)
            r = jnp.arange(d)
            j = c[:, None] + nb - a[None, :]
            pos = b0 - w + a[None, :] * d + r[:, None]
            valid = ((j >= 0) & (j <= nb))[None] & (pos >= 0)[:, None, :]
            bias_ca = bias[jnp.clip(j, 0, nb)]
            s = s + jnp.transpose(bias_ca, (2, 0, 1))[:, None]
            s = jnp.where(valid, s, -jnp.inf)
            m = jnp.max(s, axis=-1)
            p = jnp.exp(s - m[..., None])
            den = jnp.sum(p, axis=-1)
            num = jnp.einsum('bhrca,barhe->bcrhe', p, vr.astype(f32)).reshape(B, Q_BLOCK, H, E)
            den_q = jnp.transpose(den, (0, 3, 2, 1)).reshape(B, Q_BLOCK, H)
            m_q = jnp.transpose(m, (0, 3, 2, 1)).reshape(B, Q_BLOCK, H)
            parts.append((num, den_q, m_q))
        return combine_groups(parts)

    out = lax.map(block, jnp.arange(S // Q_BLOCK))
    return jnp.transpose(out, (1, 0, 2, 3, 4)).reshape(B, S, H, E)


def dilated_attention_sample(q, k_all, v_all, biases):
    B, T, H, E = q.shape
    f32 = jnp.float32
    qi = k_all.shape[1] - T + jnp.arange(T)
    parts = []
    for (w, d), bias in zip(PATTERNS, biases):
        nb = w // d
        idx = qi[:, None] - d * jnp.arange(nb + 1)[None, :]
        valid = idx >= 0
        idc = jnp.clip(idx, 0)
        kg = k_all[:, idc]
        vg = v_all[:, idc]
        s = jnp.einsum('bthe,btjhe->bhtj', q, kg).astype(f32) + bias.T[None, :, None, :]
        s = jnp.where(valid[None, None], s, -jnp.inf)
        m = jnp.max(s, axis=-1)
        p = jnp.exp(s - m[..., None])
        den = jnp.sum(p, axis=-1)
        num = jnp.einsum('bhtj,btjhe->bthe', p, vg.astype(f32))
        parts.append((num, jnp.transpose(den, (0, 2, 1)), jnp.transpose(m, (0, 2, 1))))
    return combine_groups(parts)


def mlstm_chunk(state, xs):
    C0, n0, m0 = state
    q, k, v, ig, lf = xs
    T = q.shape[1]
    F = jnp.transpose(jnp.cumsum(lf, axis=1), (0, 2, 1))
    igh = jnp.transpose(ig, (0, 2, 1))
    causal = jnp.tril(jnp.ones((T, T), dtype=bool))
    logD = jnp.where(causal, F[..., :, None] - F[..., None, :] + igh[..., None, :], -jnp.inf)
    lstate = F + m0[..., None]
    m = jnp.maximum(lstate, jnp.max(logD, axis=-1))
    wgt = jnp.exp(logD - m[..., None]) * jnp.einsum('bthk,bshk->bhts', q, k)
    sc = jnp.exp(lstate - m)
    num = jnp.einsum('bhts,bshv->bthv', wgt, v) + jnp.transpose(sc, (0, 2, 1))[..., None] * jnp.einsum('bthk,bhkv->bthv', q, C0)
    den = jnp.sum(wgt, axis=-1) + sc * jnp.einsum('bthk,bhk->bht', q, n0)
    h = num / jnp.transpose(jnp.maximum(jnp.abs(den), jnp.exp(-m)), (0, 2, 1))[..., None]
    FT = F[..., -1]
    lT = FT + m0
    logE = FT[..., None] - F + igh
    m_new = jnp.maximum(lT, jnp.max(logE, axis=-1))
    e = jnp.exp(logE - m_new[..., None])
    sT = jnp.exp(lT - m_new)
    C_new = sT[..., None, None] * C0 + jnp.einsum('bhs,bshk,bshv->bhkv', e, k, v)
    n_new = sT[..., None] * n0 + jnp.einsum('bhs,bshk->bhk', e, k)
    return (C_new, n_new, m_new), h


def mlstm(q, k, v, ig, lf, C0, n0, m0):
    B, T = q.shape[0], q.shape[1]
    cl = CHUNK if T % CHUNK == 0 else T
    nc = T // cl

    def to_chunks(t):
        return jnp.swapaxes(t.reshape((B, nc, cl) + t.shape[2:]), 0, 1)

    state, hs = lax.scan(mlstm_chunk, (C0, n0, m0), (to_chunks(q), to_chunks(k), to_chunks(v), to_chunks(ig), to_chunks(lf)))
    return jnp.swapaxes(hs, 0, 1).reshape(B, T, M_HEADS, M_DV), state


def causal_conv(x, buf, w, b):
    T = x.shape[1]
    xx = jnp.concatenate([buf.astype(x.dtype), x], axis=1)
    y = b + xx[:, 0:T] * w[0]
    for i in range(1, CONV_W):
        y = y + xx[:, i:i + T] * w[i]
    return y, xx[:, T:]


def token_mixer(h, w_in, conv_w, conv_b, b_if, m_norm, w_pm, w_pa, w_out, biases, state):
    B, T, _ = h.shape
    f32 = jnp.float32
    dt = h.dtype
    qk_pre, mv, mo, mif, aq, ak, av, gm, ga = split_cols(h @ w_in)
    if state is None:
        conv_buf = jnp.zeros((B, CONV_W - 1, QK_WIDTH), qk_pre.dtype)
        C0 = jnp.zeros((B, M_HEADS, M_DK, M_DV), f32)
        n0 = jnp.zeros((B, M_HEADS, M_DK), f32)
        m0 = jnp.zeros((B, M_HEADS), f32)
    else:
        k_buf, v_buf, C0, n0, m0, conv_buf = state
        C0, n0, m0 = C0.astype(f32), n0.astype(f32), m0.astype(f32)
    qk, conv_new = causal_conv(qk_pre, conv_buf, conv_w, conv_b)
    qk = jax.nn.silu(qk).astype(f32)
    mq = qk[..., :QK_WIDTH // 2].reshape(B, T, M_HEADS, M_DK)
    mk = qk[..., QK_WIDTH // 2:].reshape(B, T, M_HEADS, M_DK) * (M_DK ** -0.5)
    mvv = mv.astype(f32).reshape(B, T, M_HEADS, M_DV)
    gif = mif.astype(f32) + b_if.astype(f32)
    ig = gif[..., :M_HEADS]
    lf = jax.nn.log_sigmoid(gif[..., M_HEADS:])
    hm, (C1, n1, m1) = mlstm(mq, mk, mvv, ig, lf, C0, n0, m0)
    hm = hm * lax.rsqrt(jnp.mean(hm * hm, axis=-1, keepdims=True) + EPS)
    hm = jax.nn.sigmoid(mo.astype(f32)) * hm.reshape(B, T, M_WIDTH) * m_norm.astype(f32)
    aq = aq.reshape(B, T, A_HEADS, A_DH) * (A_DH ** -0.5)
    ak = ak.reshape(B, T, A_HEADS, A_DH)
    av = av.reshape(B, T, A_HEADS, A_DH)
    if state is None:
        ha = dilated_attention_prompt(aq, ak, av, biases)
        keep = min(MAX_WINDOW, T)
        k_new, v_new = ak[:, T - keep:], av[:, T - keep:]
    else:
        k_all = jnp.concatenate([k_buf.astype(ak.dtype), ak], axis=1)
        v_all = jnp.concatenate([v_buf.astype(av.dtype), av], axis=1)
        ha = dilated_attention_sample(aq, k_all, v_all, biases)
        k_new, v_new = ak, av
    merged = jax.nn.sigmoid(gm) * (hm.astype(dt) @ w_pm) + jax.nn.sigmoid(ga) * (ha.reshape(B, T, A_WIDTH).astype(dt) @ w_pa)
    return merged @ w_out, (k_new, v_new, C1, n1, m1, conv_new)


def trunk(x, params, biases, states):
    (w_in, conv_w, conv_b, b_if, m_norm, w_pm, w_pa, w_out, ln_ffa, w_ffa_in, w_ffa_out,
     ln_mix, ln_ffb, w_ffb_in, w_ffb_out, ln_f) = params
    new_states = []
    for l in range(DEPTH):
        x = x + 0.5 * swiglu(rmsnorm(x, ln_ffa[l]), w_ffa_in[l], w_ffa_out[l])
        st = None if states is None else tuple(s[l] for s in states)
        y, st_new = token_mixer(rmsnorm(x, ln_mix[l]), w_in[l], conv_w[l], conv_b[l], b_if[l], m_norm[l],
                                w_pm[l], w_pa[l], w_out[l], biases, st)
        x = x + y
        x = x + 0.5 * swiglu(rmsnorm(x, ln_ffb[l]), w_ffb_in[l], w_ffb_out[l])
        new_states.append(st_new)
    stacked = tuple(jnp.stack(s, axis=0) for s in zip(*new_states))
    return rmsnorm(x, ln_f), stacked


def setup_inputs(seed: int = 0) -> dict:
    key = jax.random.key(seed)
    ks = jax.random.split(key, 32)
    f32 = jnp.float32

    def nrm(k, shape, scale):
        return scale * jax.random.normal(k, shape, f32)

    win = min(MAX_WINDOW, PAST_LEN)
    i_bias = nrm(ks[11], (DEPTH, M_HEADS), 0.1)
    f_bias = jnp.linspace(3.0, 6.0, M_HEADS, dtype=f32)[None, :] + nrm(ks[12], (DEPTH, M_HEADS), 0.1)
    return {
        'x_prompt': nrm(ks[0], (BATCH, SEQ, D_MODEL), 1.0),
        'x_sample': nrm(ks[1], (DEC_BATCH, DEC_SEQ, D_MODEL), 1.0),
        'cache_k': nrm(ks[2], (DEPTH, DEC_BATCH, win, A_HEADS, A_DH), 1.0),
        'cache_v': nrm(ks[3], (DEPTH, DEC_BATCH, win, A_HEADS, A_DH), 1.0),
        'state_C': nrm(ks[4], (DEPTH, DEC_BATCH, M_HEADS, M_DK, M_DV), 0.05),
        'state_n': nrm(ks[5], (DEPTH, DEC_BATCH, M_HEADS, M_DK), 0.05),
        'state_m': nrm(ks[6], (DEPTH, DEC_BATCH, M_HEADS), 0.5),
        'state_conv': nrm(ks[7], (DEPTH, DEC_BATCH, CONV_W - 1, QK_WIDTH), 1.0),
        'w_in': nrm(ks[8], (DEPTH, D_MODEL, N_IN), D_MODEL ** -0.5),
        'conv_w': nrm(ks[9], (DEPTH, CONV_W, QK_WIDTH), CONV_W ** -0.5),
        'conv_b': nrm(ks[10], (DEPTH, QK_WIDTH), 0.01),
        'b_if': jnp.concatenate([i_bias, f_bias], axis=-1),
        'm_norm': 1.0 + nrm(ks[13], (DEPTH, M_WIDTH), 0.01),
        'w_pm': nrm(ks[14], (DEPTH, M_WIDTH, D_MODEL), M_WIDTH ** -0.5),
        'w_pa': nrm(ks[15], (DEPTH, A_WIDTH, D_MODEL), A_WIDTH ** -0.5),
        'w_out': nrm(ks[16], (DEPTH, D_MODEL, D_MODEL), D_MODEL ** -0.5),
        'rel_table': nrm(ks[17], (N_BUCKETS, A_HEADS), 0.5),
        'ln_ffa': 1.0 + nrm(ks[18], (DEPTH, D_MODEL), 0.01),
        'w_ffa_in': nrm(ks[19], (DEPTH, D_MODEL, 2 * D_FF), D_MODEL ** -0.5),
        'w_ffa_out': nrm(ks[20], (DEPTH, D_FF, D_MODEL), D_FF ** -0.5),
        'ln_mix': 1.0 + nrm(ks[21], (DEPTH, D_MODEL), 0.01),
        'ln_ffb': 1.0 + nrm(ks[22], (DEPTH, D_MODEL), 0.01),
        'w_ffb_in': nrm(ks[23], (DEPTH, D_MODEL, 2 * D_FF), D_MODEL ** -0.5),
        'w_ffb_out': nrm(ks[24], (DEPTH, D_FF, D_MODEL), D_FF ** -0.5),
        'ln_f': 1.0 + nrm(ks[25], (D_MODEL,), 0.01),
    }


def reference(x_prompt, x_sample, cache_k, cache_v, state_C, state_n, state_m, state_conv,
              w_in, conv_w, conv_b, b_if, m_norm, w_pm, w_pa, w_out, rel_table,
              ln_ffa, w_ffa_in, w_ffa_out, ln_mix, ln_ffb, w_ffb_in, w_ffb_out, ln_f):
    params = (w_in, conv_w, conv_b, b_if, m_norm, w_pm, w_pa, w_out, ln_ffa, w_ffa_in, w_ffa_out,
              ln_mix, ln_ffb, w_ffb_in, w_ffb_out, ln_f)
    biases = pattern_biases(rel_table)
    y_prompt, (k_p, v_p, C_p, n_p, m_p, conv_p) = trunk(x_prompt, params, biases, None)
    y_sample, (k_s, v_s, C_s, n_s, m_s, conv_s) = trunk(
        x_sample, params, biases, (cache_k, cache_v, state_C, state_n, state_m, state_conv))
    return (y_prompt, y_sample, k_p, v_p, C_p, n_p, m_p, conv_p, k_s, v_s, C_s, n_s, m_s, conv_s)
```

```python
import functools
import math

import jax
import jax.numpy as jnp
from jax import lax
from jax.experimental import pallas as pl
from jax.experimental.pallas import tpu as pltpu

F32 = jnp.float32
BF16 = jnp.bfloat16

M_HEADS = 4
M_DK = 256
M_DV = 256
M_WIDTH = M_HEADS * M_DV
QK_WIDTH = 2 * M_HEADS * M_DK
CONV_W = 4
CHUNK = 128
A_HEADS = 8
A_DH = 128
A_WIDTH = A_HEADS * A_DH
PATTERNS = ((128, 1), (512, 4), (2048, 16))
MAX_WINDOW = 2048
Q_BLOCK = 128
N_BUCKETS = 32
MAX_DISTANCE = MAX_WINDOW
EPS = 1e-6
N_STEPS = 128
LANE = 128
CONV_PAD = 8
VMEM_LIMIT = 58 * 1024 * 1024


def _dot(a, b):
    return jnp.dot(a, b, preferred_element_type=F32)


def _dot_nt(a, b):
    return lax.dot_general(a, b, (((1,), (1,)), ((), ())), preferred_element_type=F32)


def _dot_tn(a, b):
    return lax.dot_general(a, b, (((0,), (0,)), ((), ())), preferred_element_type=F32)


def _rms(x, g):
    return x * lax.rsqrt(jnp.mean(x * x, axis=-1, keepdims=True) + EPS) * g


def _params(sem):
    return pltpu.CompilerParams(dimension_semantics=sem, vmem_limit_bytes=VMEM_LIMIT)


def _ffn_kernel(x_ref, g_ref, wg_ref, wu_ref, wo_ref, o_ref, h_ref):
    j = pl.program_id(1)

    @pl.when(j == 0)
    def _():
        h_ref[...] = _rms(x_ref[...], g_ref[...]).astype(BF16)
        o_ref[...] = jnp.zeros_like(o_ref)

    h = h_ref[...]
    gate = _dot(h, wg_ref[...])
    up = _dot(h, wu_ref[...])
    act = (gate * jax.nn.sigmoid(gate) * up).astype(BF16)
    o_ref[...] += _dot(act, wo_ref[...])

    @pl.when(j == pl.num_programs(1) - 1)
    def _():
        o_ref[...] = x_ref[...] + 0.5 * o_ref[...]


def _ffn(x, g, w_in, w_out, l, tm, tf):
    M, D = x.shape
    FF = w_out.shape[1]
    nj = FF // tf
    return pl.pallas_call(
        _ffn_kernel,
        grid=(M // tm, nj),
        in_specs=[
            pl.BlockSpec((tm, D), lambda i, j: (i, 0)),
            pl.BlockSpec((None, 1, D), lambda i, j: (l, 0, 0)),
            pl.BlockSpec((None, D, tf), lambda i, j: (l, 0, j)),
            pl.BlockSpec((None, D, tf), lambda i, j: (l, 0, j + nj)),
            pl.BlockSpec((None, tf, D), lambda i, j: (l, j, 0)),
        ],
        out_specs=pl.BlockSpec((tm, D), lambda i, j: (i, 0)),
        out_shape=jax.ShapeDtypeStruct((M, D), F32),
        scratch_shapes=[pltpu.VMEM((tm, D), BF16)],
        compiler_params=_params(("parallel", "arbitrary")),
        name="ffn",
    )(x, g, w_in, w_in, w_out)


def _inproj_kernel(x_ref, g_ref, w_ref, wif_ref, o_ref, gif_ref, h_ref):
    j = pl.program_id(1)

    @pl.when(j == 0)
    def _():
        h32 = _rms(x_ref[...], g_ref[...])
        hb = h32.astype(BF16)
        h_ref[...] = hb
        hl = (h32 - hb.astype(F32)).astype(BF16)
        gif_ref[...] = _dot(hb, wif_ref[0]) + _dot(hb, wif_ref[1]) + _dot(hl, wif_ref[0])

    o_ref[...] = _dot(h_ref[...], w_ref[...])


def _inproj(x, g, w_main, w_if, l, tm, tn):
    M, D = x.shape
    NP = w_main.shape[2]
    return pl.pallas_call(
        _inproj_kernel,
        grid=(M // tm, NP // tn),
        in_specs=[
            pl.BlockSpec((tm, D), lambda i, j: (i, 0)),
            pl.BlockSpec((None, 1, D), lambda i, j: (l, 0, 0)),
            pl.BlockSpec((None, D, tn), lambda i, j: (l, 0, j)),
            pl.BlockSpec((None, 2, D, LANE), lambda i, j: (l, 0, 0, 0)),
        ],
        out_specs=[
            pl.BlockSpec((tm, tn), lambda i, j: (i, j)),
            pl.BlockSpec((tm, LANE), lambda i, j: (i, 0)),
        ],
        out_shape=[jax.ShapeDtypeStruct((M, NP), F32), jax.ShapeDtypeStruct((M, LANE), F32)],
        scratch_shapes=[pltpu.VMEM((tm, D), BF16)],
        compiler_params=_params(("parallel", "arbitrary")),
        name="inproj",
    )(x, g, w_main, w_if)


def _mlstm_kernel(qp_ref, kp_ref, v_ref, og_ref, gif_ref, bif_ref, cw_ref, cb_ref, mn_ref,
                  cbuf_ref, C0_ref, n0_ref, m0_ref,
                  hm_ref, C_ref, n_ref, m_ref, xq_ref, xk_ref, *, L, mm_dtype):
    c = pl.program_id(1)
    half = QK_WIDTH // 2

    @pl.when(c == 0)
    def _():
        xq_ref[0:CONV_PAD, :] = cbuf_ref[:, 0:half]
        xk_ref[0:CONV_PAD, :] = cbuf_ref[:, half:QK_WIDTH]
        C_ref[...] = C0_ref[...]
        n_ref[...] = n0_ref[...]
        m_ref[...] = m0_ref[...]

    @pl.when(c > 0)
    def _():
        xq_ref[0:CONV_PAD, :] = xq_ref[L:L + CONV_PAD, :]
        xk_ref[0:CONV_PAD, :] = xk_ref[L:L + CONV_PAD, :]

    xq_ref[CONV_PAD:CONV_PAD + L, :] = qp_ref[...]
    xk_ref[CONV_PAD:CONV_PAD + L, :] = kp_ref[...]

    cw = cw_ref[...]
    cb = cb_ref[...]

    def conv_silu(x_ref, w, b):
        base = CONV_PAD - (CONV_W - 1)
        y = b + x_ref[pl.ds(base, L), :] * w[0:1, :]
        for i in range(1, CONV_W):
            y = y + x_ref[pl.ds(base + i, L), :] * w[i:i + 1, :]
        return y * jax.nn.sigmoid(y)

    qa = conv_silu(xq_ref, cw[:, 0:half], cb[:, 0:half])
    ka = conv_silu(xk_ref, cw[:, half:QK_WIDTH], cb[:, half:QK_WIDTH]) * (M_DK ** -0.5)
    g = gif_ref[...] + bif_ref[...]

    ti = lax.broadcasted_iota(jnp.int32, (L, L), 0)
    si = lax.broadcasted_iota(jnp.int32, (L, L), 1)
    eye = ti == si
    low = si <= ti
    upp = ti <= si

    for h in range(M_HEADS):
        sl = slice(h * M_DV, (h + 1) * M_DV)
        ig_col = g[:, h:h + 1]
        gf = g[:, M_HEADS + h:M_HEADS + h + 1]
        lf_col = jnp.minimum(gf, 0.0) - jnp.log1p(jnp.exp(-jnp.abs(gf)))
        ig_row = jnp.sum(jnp.where(eye, ig_col, 0.0), axis=0, keepdims=True)
        lf_row = jnp.sum(jnp.where(eye, lf_col, 0.0), axis=0, keepdims=True)
        F_col = jnp.sum(jnp.where(low, lf_row, 0.0), axis=1, keepdims=True)
        F_row = jnp.sum(jnp.where(upp, lf_col, 0.0), axis=0, keepdims=True)
        m0 = m_ref[h]
        C0 = C_ref[h]
        n0 = n_ref[h]
        logD = jnp.where(low, F_col - F_row + ig_row, -jnp.inf)
        lst = F_col + m0
        m_col = jnp.maximum(lst, jnp.max(logD, axis=1, keepdims=True))
        q = qa[:, sl]
        k = ka[:, sl]
        qb = q.astype(mm_dtype)
        kb = k.astype(mm_dtype)
        vb = v_ref[:, sl].astype(mm_dtype)
        wgt = jnp.exp(logD - m_col) * _dot_nt(qb, kb)
        sc = jnp.exp(lst - m_col)
        num = _dot(wgt.astype(mm_dtype), vb) + sc * _dot(qb, C0.astype(mm_dtype))
        den = jnp.sum(wgt, axis=1, keepdims=True) + sc * jnp.sum(q * n0, axis=1, keepdims=True)
        hh = num / jnp.maximum(jnp.abs(den), jnp.exp(-m_col))
        FT = F_col[L - 1:L, :]
        lT = FT + m0
        m_new = jnp.maximum(lT, jnp.max(FT - F_row + ig_row, axis=1, keepdims=True))
        e_col = jnp.exp(FT - F_col + ig_col - m_new)
        sT = jnp.exp(lT - m_new)
        ke = k * e_col
        C_ref[h] = sT * C0 + _dot_tn(ke.astype(mm_dtype), vb)
        n_ref[h] = sT * n0 + jnp.sum(ke, axis=0, keepdims=True)
        m_ref[h] = m_new
        hn = hh * lax.rsqrt(jnp.mean(hh * hh, axis=1, keepdims=True) + EPS)
        out = jax.nn.sigmoid(og_ref[:, sl]) * hn * mn_ref[:, sl]
        hm_ref[:, sl] = out.astype(hm_ref.dtype)


def _mlstm(proj, gif, b_if, conv_w, conv_b, m_norm, cbuf, C0, n0, m0, l, ls, B, T, col0, out_dtype, mm_dtype):
    L = CHUNK if T % CHUNK == 0 else T
    nc = T // L
    qo = col0 // M_WIDTH
    kern = functools.partial(_mlstm_kernel, L=L, mm_dtype=mm_dtype)
    row = lambda b, c: b * nc + c
    return pl.pallas_call(
        kern,
        grid=(B, nc),
        in_specs=[
            pl.BlockSpec((L, M_WIDTH), lambda b, c: (row(b, c), qo)),
            pl.BlockSpec((L, M_WIDTH), lambda b, c: (row(b, c), qo + 1)),
            pl.BlockSpec((L, M_WIDTH), lambda b, c: (row(b, c), qo + 2)),
            pl.BlockSpec((L, M_WIDTH), lambda b, c: (row(b, c), qo + 3)),
            pl.BlockSpec((L, LANE), lambda b, c: (row(b, c), 0)),
            pl.BlockSpec((None, 1, LANE), lambda b, c: (l, 0, 0)),
            pl.BlockSpec((None, CONV_W, QK_WIDTH), lambda b, c: (l, 0, 0)),
            pl.BlockSpec((None, 1, QK_WIDTH), lambda b, c: (l, 0, 0)),
            pl.BlockSpec((None, 1, M_WIDTH), lambda b, c: (l, 0, 0)),
            pl.BlockSpec((None, None, CONV_PAD, QK_WIDTH), lambda b, c: (ls, b, 0, 0)),
            pl.BlockSpec((None, None, M_HEADS, M_DK, M_DV), lambda b, c: (ls, b, 0, 0, 0)),
            pl.BlockSpec((None, None, M_HEADS, 1, M_DK), lambda b, c: (ls, b, 0, 0, 0)),
            pl.BlockSpec((None, None, M_HEADS, 1, 1), lambda b, c: (ls, b, 0, 0, 0)),
        ],
        out_specs=[
            pl.BlockSpec((L, M_WIDTH), lambda b, c: (row(b, c), 0)),
            pl.BlockSpec((None, M_HEADS, M_DK, M_DV), lambda b, c: (b, 0, 0, 0)),
            pl.BlockSpec((None, M_HEADS, 1, M_DK), lambda b, c: (b, 0, 0, 0)),
            pl.BlockSpec((None, M_HEADS, 1, 1), lambda b, c: (b, 0, 0, 0)),
        ],
        out_shape=[
            jax.ShapeDtypeStruct((B * T, M_WIDTH), out_dtype),
            jax.ShapeDtypeStruct((B, M_HEADS, M_DK, M_DV), F32),
            jax.ShapeDtypeStruct((B, M_HEADS, 1, M_DK), F32),
            jax.ShapeDtypeStruct((B, M_HEADS, 1, 1), F32),
        ],
        scratch_shapes=[pltpu.VMEM((L + CONV_PAD, M_WIDTH), F32), pltpu.VMEM((L + CONV_PAD, M_WIDTH), F32)],
        compiler_params=_params(("parallel", "arbitrary")),
        name="mlstm",
    )(proj, proj, proj, proj, gif, b_if, conv_w, conv_b, m_norm, cbuf, C0, n0, m0)


ATTN_UNROLL = 4


def _attn_kernel(q_ref, k_ref, v_ref, bias_ref, o_ref, qp, kp, vp, nump, stp, numn, stn, *, S):
    scale = A_DH ** -0.5
    QB = Q_BLOCK
    nunits = S // QB
    lane = lax.broadcasted_iota(jnp.int32, (QB, LANE), 1)
    kp[0:QB, :] = jnp.zeros((QB, A_DH), BF16)
    vp[0:QB, :] = jnp.zeros((QB, A_DH), BF16)

    for p, (w, d) in enumerate(PATTERNS):
        Ls = S // d
        nblk = Ls // QB
        for r in range(d):
            rows = pl.ds(r, Ls, stride=d) if d > 1 else pl.ds(0, Ls)
            qp[r * Ls:(r + 1) * Ls, :] = (q_ref[rows, :] * scale).astype(BF16)
            kp[QB + r * Ls:QB + (r + 1) * Ls, :] = k_ref[rows, :].astype(BF16)
            vp[QB + r * Ls:QB + (r + 1) * Ls, :] = v_ref[rows, :].astype(BF16)

        def unit(u, p=p, nblk=nblk):
            row0 = pl.multiple_of(u * QB, QB)
            qb = qp[pl.ds(row0, QB), :]
            if nblk == 1:
                kw = kp[pl.ds(row0 + QB, QB), :]
                vw = vp[pl.ds(row0 + QB, QB), :]
                bias = bias_ref[p, 0, :, QB:2 * QB]
            else:
                kw = kp[pl.ds(row0, 2 * QB), :]
                vw = vp[pl.ds(row0, 2 * QB), :]
                first = (lax.rem(u, nblk) == 0).astype(jnp.int32)
                bias = bias_ref[p, first]
            s = _dot_nt(qb, kw) + bias
            m = jnp.max(s, axis=1, keepdims=True)
            pr = jnp.exp(s - m)
            den = jnp.sum(pr, axis=1, keepdims=True)
            nump[pl.ds(row0, QB), :] = _dot(pr.astype(BF16), vw)
            stp[pl.ds(row0, QB), :] = jnp.where(lane == 0, m, den)

        def body(it, carry):
            for i in range(ATTN_UNROLL):
                unit(it * ATTN_UNROLL + i)
            return carry

        lax.fori_loop(0, nunits // ATTN_UNROLL, body, 0)

        for r in range(d):
            rows = pl.ds(r, Ls, stride=d) if d > 1 else pl.ds(0, Ls)
            numn[p, rows, :] = nump[r * Ls:(r + 1) * Ls, :]
            stn[p, rows, :] = stp[r * Ls:(r + 1) * Ls, :]

    def combine(bi, carry):
        rows = pl.ds(pl.multiple_of(bi * QB, QB), QB)
        sts = [stn[p, rows, :] for p in range(len(PATTERNS))]
        ms = [st[:, 0:1] for st in sts]
        dens = [st[:, 1:2] for st in sts]
        mx = functools.reduce(jnp.maximum, ms)
        tot = jnp.zeros((QB, A_DH), F32)
        wsum = jnp.zeros((QB, 1), F32)
        for p in range(len(PATTERNS)):
            a = jnp.exp(ms[p] - mx)
            tot = tot + a * numn[p, rows, :]
            wsum = wsum + a * dens[p]
        o_ref[rows, :] = (tot / wsum).astype(o_ref.dtype)
        return carry

    lax.fori_loop(0, nunits, combine, 0)


def _attn_prompt(proj, bias, B, S, col0):
    H = A_HEADS
    cb = col0 // A_DH
    kern = functools.partial(_attn_kernel, S=S)
    npat = len(PATTERNS)
    return pl.pallas_call(
        kern,
        grid=(B, H),
        in_specs=[
            pl.BlockSpec((S, A_DH), lambda b, h: (b, cb + h)),
            pl.BlockSpec((S, A_DH), lambda b, h: (b, cb + H + h)),
            pl.BlockSpec((S, A_DH), lambda b, h: (b, cb + 2 * H + h)),
            pl.BlockSpec((None, npat, 2, Q_BLOCK, 2 * Q_BLOCK), lambda b, h: (h, 0, 0, 0, 0)),
        ],
        out_specs=pl.BlockSpec((S, A_DH), lambda b, h: (b, h)),
        out_shape=jax.ShapeDtypeStruct((B * S, A_WIDTH), BF16),
        scratch_shapes=[
            pltpu.VMEM((S, A_DH), BF16),
            pltpu.VMEM((S + Q_BLOCK, A_DH), BF16),
            pltpu.VMEM((S + Q_BLOCK, A_DH), BF16),
            pltpu.VMEM((S, A_DH), F32),
            pltpu.VMEM((S, LANE), F32),
            pltpu.VMEM((npat, S, A_DH), F32),
            pltpu.VMEM((npat, S, LANE), F32),
        ],
        compiler_params=_params(("parallel", "parallel")),
        name="attn_prompt",
    )(proj, proj, proj, bias)


def _attn_s_kernel(q_ref, kn_ref, vn_ref, kc_ref, vc_ref, bc_ref, bn_ref, o_ref):
    q = q_ref[...] * (A_DH ** -0.5)
    s_c = _dot_nt(q, kc_ref[...])
    s_n = _dot_nt(q, kn_ref[...])
    npat = len(PATTERNS)
    mx = None
    for p in range(npat):
        mp = jnp.maximum(jnp.max(s_c + bc_ref[p], axis=1, keepdims=True),
                         jnp.max(s_n + bn_ref[p], axis=1, keepdims=True))
        mx = mp if mx is None else jnp.maximum(mx, mp)
    e_c = jnp.exp(s_c + bc_ref[0] - mx)
    e_n = jnp.exp(s_n + bn_ref[0] - mx)
    for p in range(1, npat):
        e_c = e_c + jnp.exp(s_c + bc_ref[p] - mx)
        e_n = e_n + jnp.exp(s_n + bn_ref[p] - mx)
    num = _dot(e_c, vc_ref[...]) + _dot(e_n, vn_ref[...])
    den = jnp.sum(e_c, axis=1, keepdims=True) + jnp.sum(e_n, axis=1, keepdims=True)
    o_ref[...] = num / den


def _attn_sample(proj, cache_k, cache_v, bias_c, bias_n, l, B, T, col0):
    H = A_HEADS
    cb = col0 // A_DH
    Lbuf = cache_k.shape[2]
    npat = len(PATTERNS)
    return pl.pallas_call(
        _attn_s_kernel,
        grid=(B, H),
        in_specs=[
            pl.BlockSpec((T, A_DH), lambda b, h: (b, cb + h)),
            pl.BlockSpec((T, A_DH), lambda b, h: (b, cb + H + h)),
            pl.BlockSpec((T, A_DH), lambda b, h: (b, cb + 2 * H + h)),
            pl.BlockSpec((None, None, Lbuf, A_DH), lambda b, h: (l, b, 0, h)),
            pl.BlockSpec((None, None, Lbuf, A_DH), lambda b, h: (l, b, 0, h)),
            pl.BlockSpec((None, npat, T, Lbuf), lambda b, h: (h, 0, 0, 0)),
            pl.BlockSpec((None, npat, T, T), lambda b, h: (h, 0, 0, 0)),
        ],
        out_specs=pl.BlockSpec((T, A_DH), lambda b, h: (b, h)),
        out_shape=jax.ShapeDtypeStruct((B * T, A_WIDTH), F32),
        compiler_params=_params(("parallel", "parallel")),
        name="attn_sample",
    )(proj, proj, proj, cache_k, cache_v, bias_c, bias_n)


def _outproj_kernel(x_ref, hm_ref, ha_ref, gm_ref, ga_ref, wpm_ref, wpa_ref, wo_ref, o_ref):
    pm = _dot(hm_ref[...].astype(BF16), wpm_ref[...])
    pa = _dot(ha_ref[...].astype(BF16), wpa_ref[...])
    merged = jax.nn.sigmoid(gm_ref[...]) * pm + jax.nn.sigmoid(ga_ref[...]) * pa
    o_ref[...] = x_ref[...] + _dot(merged.astype(BF16), wo_ref[...])


def _outproj(x, hm, ha, proj, w_pm, w_pa, w_out, l, tm):
    M, D = x.shape
    once = pl.Buffered(1)
    return pl.pallas_call(
        _outproj_kernel,
        grid=(M // tm,),
        in_specs=[
            pl.BlockSpec((tm, D), lambda i: (i, 0)),
            pl.BlockSpec((tm, M_WIDTH), lambda i: (i, 0)),
            pl.BlockSpec((tm, A_WIDTH), lambda i: (i, 0)),
            pl.BlockSpec((tm, D), lambda i: (i, 0)),
            pl.BlockSpec((tm, D), lambda i: (i, 1)),
            pl.BlockSpec((None, M_WIDTH, D), lambda i: (l, 0, 0), pipeline_mode=once),
            pl.BlockSpec((None, A_WIDTH, D), lambda i: (l, 0, 0), pipeline_mode=once),
            pl.BlockSpec((None, D, D), lambda i: (l, 0, 0), pipeline_mode=once),
        ],
        out_specs=pl.BlockSpec((tm, D), lambda i: (i, 0)),
        out_shape=jax.ShapeDtypeStruct((M, D), F32),
        compiler_params=_params(("parallel",)),
        name="outproj",
    )(x, hm, ha, proj, proj, w_pm, w_pa, w_out)


def _norm_kernel(x_ref, g_ref, o_ref):
    o_ref[...] = _rms(x_ref[...], g_ref[...])


def _final_norm(x, g, tm):
    M, D = x.shape
    return pl.pallas_call(
        _norm_kernel,
        grid=(M // tm,),
        in_specs=[pl.BlockSpec((tm, D), lambda i: (i, 0)), pl.BlockSpec((1, D), lambda i: (0, 0))],
        out_specs=pl.BlockSpec((tm, D), lambda i: (i, 0)),
        out_shape=jax.ShapeDtypeStruct((M, D), F32),
        compiler_params=_params(("parallel",)),
        name="final_norm",
    )(x, g)


def _t5_bucket(dist):
    exact = N_BUCKETS // 2
    d32 = jnp.maximum(dist, 1).astype(F32)
    large = exact + (jnp.log(d32 / exact) / math.log(MAX_DISTANCE / exact) * (N_BUCKETS - exact)).astype(jnp.int32)
    large = jnp.minimum(large, N_BUCKETS - 1)
    return jnp.where(dist < exact, dist, large)


def _step_biases(rel_table):
    tab = rel_table.astype(F32)
    assert all(w // d == N_STEPS for w, d in PATTERNS)
    return jnp.stack([tab[_t5_bucket(d * jnp.arange(N_STEPS + 1, dtype=jnp.int32))] for (w, d) in PATTERNS])


def _prompt_bias(sb):
    c = jnp.arange(Q_BLOCK)[:, None]
    a = jnp.arange(2 * Q_BLOCK)[None, :]
    j = c + Q_BLOCK - a
    valid = (j >= 0) & (j <= N_STEPS)
    base = jnp.where(valid[None, :, :, None], sb[:, jnp.clip(j, 0, N_STEPS)], -jnp.inf)
    first = jnp.where((a >= Q_BLOCK)[None, :, :, None], base, -jnp.inf)
    return jnp.transpose(jnp.stack([base, first], axis=1), (4, 0, 1, 2, 3))


def _sample_bias(sb, T, Lbuf):
    qi = Lbuf + jnp.arange(T)[:, None]

    def table(key_idx):
        dist = qi - key_idx
        outs = []
        for p, (w, d) in enumerate(PATTERNS):
            j = dist // d
            valid = (dist >= 0) & (dist % d == 0) & (j <= N_STEPS)
            outs.append(jnp.where(valid[:, :, None], sb[p][jnp.clip(j, 0, N_STEPS)], -jnp.inf))
        return jnp.transpose(jnp.stack(outs), (3, 0, 1, 2))

    return table(jnp.arange(Lbuf)[None, :]), table(Lbuf + jnp.arange(T)[None, :])


def _row_tile(M, cap):
    return cap if M % cap == 0 else M


def _trunk(x3, wts, sb, states):
    B, T, D = x3.shape
    M = B * T
    depth = wts["w_main"].shape[0]
    prompt = states is None
    FF = wts["w_ffa_out"].shape[1]
    tm = _row_tile(M, 1024)
    tf = 512 if FF % 512 == 0 else FF
    tn = 1024
    tm_out = _row_tile(M, 512)
    col_m = 2 * D
    col_a = 2 * D + QK_WIDTH + 2 * M_WIDTH
    if prompt:
        bias = _prompt_bias(sb)
        cbuf = jnp.zeros((1, B, CONV_PAD, QK_WIDTH), F32)
        C0 = jnp.zeros((1, B, M_HEADS, M_DK, M_DV), F32)
        n0 = jnp.zeros((1, B, M_HEADS, 1, M_DK), F32)
        m0 = jnp.zeros((1, B, M_HEADS, 1, 1), F32)
    else:
        cache_k, cache_v, sC, sn, sm, sconv = states
        Lbuf = cache_k.shape[2]
        bias_c, bias_n = _sample_bias(sb, T, Lbuf)
        cache_k = cache_k.reshape(depth, B, Lbuf, A_WIDTH)
        cache_v = cache_v.reshape(depth, B, Lbuf, A_WIDTH)
        cbuf = jnp.pad(sconv.astype(F32), ((0, 0), (0, 0), (CONV_PAD - (CONV_W - 1), 0), (0, 0)))
        C0 = sC.astype(F32)
        n0 = sn.astype(F32).reshape(depth, B, M_HEADS, 1, M_DK)
        m0 = sm.astype(F32).reshape(depth, B, M_HEADS, 1, 1)

    x = x3.reshape(M, D)
    ks, vs, Cs, ns, ms, convs = [], [], [], [], [], []
    for l in range(depth):
        x = _ffn(x, wts["ln_ffa"], wts["w_ffa_in"], wts["w_ffa_out"], l, tm, tf)
        proj, gif = _inproj(x, wts["ln_mix"], wts["w_main"], wts["w_if"], l, tm, tn)
        ls = 0 if prompt else l
        hm, C1, n1, m1 = _mlstm(proj, gif, wts["b_if"], wts["conv_w"], wts["conv_b"], wts["m_norm"],
                                cbuf, C0, n0, m0, l, ls, B, T, col_m,
                                BF16 if prompt else F32, BF16 if prompt else F32)
        if prompt:
            ha = _attn_prompt(proj, bias, B, T, col_a)
        else:
            ha = _attn_sample(proj, cache_k, cache_v, bias_c, bias_n, l, B, T, col_a)
        x = _outproj(x, hm, ha, proj, wts["w_pm"], wts["w_pa"], wts["w_out"], l, tm_out)
        x = _ffn(x, wts["ln_ffb"], wts["w_ffb_in"], wts["w_ffb_out"], l, tm, tf)
        p3 = proj.reshape(B, T, -1)
        keep = min(MAX_WINDOW, T)
        ks.append(p3[:, T - keep:, col_a + A_WIDTH:col_a + 2 * A_WIDTH].reshape(B, keep, A_HEADS, A_DH))
        vs.append(p3[:, T - keep:, col_a + 2 * A_WIDTH:col_a + 3 * A_WIDTH].reshape(B, keep, A_HEADS, A_DH))
        Cs.append(C1)
        ns.append(n1.reshape(B, M_HEADS, M_DK))
        ms.append(m1.reshape(B, M_HEADS))
        pre = p3[:, :, col_m:col_m + QK_WIDTH]
        if T >= CONV_W - 1:
            convs.append(pre[:, T - (CONV_W - 1):])
        else:
            convs.append(jnp.concatenate([cbuf[ls][:, CONV_PAD - (CONV_W - 1) + T:], pre], axis=1))
    y = _final_norm(x, wts["ln_f"], tm).reshape(B, T, D)
    stk = lambda xs: jnp.stack(xs, axis=0)
    return y, (stk(ks), stk(vs), stk(Cs), stk(ns), stk(ms), stk(convs))


def kernel(x_prompt, x_sample, cache_k, cache_v, state_C, state_n, state_m, state_conv, w_in, conv_w, conv_b, b_if, m_norm, w_pm, w_pa, w_out, rel_table, ln_ffa, w_ffa_in, w_ffa_out, ln_mix, ln_ffb, w_ffb_in, w_ffb_out, ln_f):
    depth, D, _ = w_in.shape
    o_if = QK_WIDTH + 2 * M_WIDTH
    o_a = o_if + 2 * M_HEADS
    o_g = o_a + 3 * A_WIDTH
    w_main = jnp.concatenate([w_in[:, :, o_g:], w_in[:, :, :o_if], w_in[:, :, o_a:o_g]], axis=-1).astype(BF16)
    w_if32 = jnp.pad(w_in[:, :, o_if:o_a], ((0, 0), (0, 0), (0, LANE - 2 * M_HEADS)))
    w_if_hi = w_if32.astype(BF16)
    w_if_lo = (w_if32 - w_if_hi.astype(F32)).astype(BF16)
    wts = dict(
        w_main=w_main,
        w_if=jnp.stack([w_if_hi, w_if_lo], axis=1),
        b_if=jnp.pad(b_if.astype(F32), ((0, 0), (0, LANE - 2 * M_HEADS))).reshape(depth, 1, LANE),
        conv_w=conv_w.astype(F32),
        conv_b=conv_b.astype(F32).reshape(depth, 1, QK_WIDTH),
        m_norm=m_norm.astype(F32).reshape(depth, 1, M_WIDTH),
        w_pm=w_pm.astype(BF16), w_pa=w_pa.astype(BF16), w_out=w_out.astype(BF16),
        ln_ffa=ln_ffa.astype(F32).reshape(depth, 1, D), ln_mix=ln_mix.astype(F32).reshape(depth, 1, D),
        ln_ffb=ln_ffb.astype(F32).reshape(depth, 1, D), ln_f=ln_f.astype(F32).reshape(1, D),
        w_ffa_in=w_ffa_in.astype(BF16), w_ffa_out=w_ffa_out.astype(BF16),
        w_ffb_in=w_ffb_in.astype(BF16), w_ffb_out=w_ffb_out.astype(BF16),
    )
    sb = _step_biases(rel_table)
    y_p, (k_p, v_p, C_p, n_p, m_p, conv_p) = _trunk(x_prompt, wts, sb, None)
    y_s, (k_s, v_s, C_s, n_s, m_s, conv_s) = _trunk(
        x_sample, wts, sb, (cache_k, cache_v, state_C, state_n, state_m, state_conv))
    return (y_p, y_s, k_p, v_p, C_p, n_p, m_p, conv_p, k_s, v_s, C_s, n_s, m_s, conv_s)
```

```python
import functools
import math

import jax
import jax.numpy as jnp
from jax import lax
from jax.experimental import pallas as pl
from jax.experimental.pallas import tpu as pltpu

F32 = jnp.float32
BF16 = jnp.bfloat16

M_HEADS = 4
M_DK = 256
M_DV = 256
M_WIDTH = M_HEADS * M_DV
QK_WIDTH = 2 * M_HEADS * M_DK
CONV_W = 4
CHUNK = 128
A_HEADS = 8
A_DH = 128
A_WIDTH = A_HEADS * A_DH
PATTERNS = ((128, 1), (512, 4), (2048, 16))
MAX_WINDOW = 2048
Q_BLOCK = 128
N_BUCKETS = 32
MAX_DISTANCE = MAX_WINDOW
EPS = 1e-6
N_STEPS = 128
LANE = 128
CONV_PAD = 8
VMEM_LIMIT = 58 * 1024 * 1024


def _dot(a, b):
    return jnp.dot(a, b, preferred_element_type=F32)


def _dot_nt(a, b):
    return lax.dot_general(a, b, (((1,), (1,)), ((), ())), preferred_element_type=F32)


def _dot_tn(a, b):
    return lax.dot_general(a, b, (((0,), (0,)), ((), ())), preferred_element_type=F32)


def _rms(x, g):
    return x * lax.rsqrt(jnp.mean(x * x, axis=-1, keepdims=True) + EPS) * g


def _params(sem):
    return pltpu.CompilerParams(dimension_semantics=sem, vmem_limit_bytes=VMEM_LIMIT)


def _ffn_kernel(x_ref, g_ref, wg_ref, wu_ref, wo_ref, o_ref, h_ref):
    j = pl.program_id(1)

    @pl.when(j == 0)
    def _():
        h_ref[...] = _rms(x_ref[...], g_ref[...]).astype(BF16)
        o_ref[...] = jnp.zeros_like(o_ref)

    h = h_ref[...]
    gate = _dot(h, wg_ref[...])
    up = _dot(h, wu_ref[...])
    act = (gate * jax.nn.sigmoid(gate) * up).astype(BF16)
    o_ref[...] += _dot(act, wo_ref[...])

    @pl.when(j == pl.num_programs(1) - 1)
    def _():
        o_ref[...] = x_ref[...] + 0.5 * o_ref[...]


def _ffn(x, g, w_in, w_out, l, tm, tf):
    M, D = x.shape
    FF = w_out.shape[1]
    nj = FF // tf
    return pl.pallas_call(
        _ffn_kernel,
        grid=(M // tm, nj),
        in_specs=[
            pl.BlockSpec((tm, D), lambda i, j: (i, 0)),
            pl.BlockSpec((None, 1, D), lambda i, j: (l, 0, 0)),
            pl.BlockSpec((None, D, tf), lambda i, j: (l, 0, j)),
            pl.BlockSpec((None, D, tf), lambda i, j: (l, 0, j + nj)),
            pl.BlockSpec((None, tf, D), lambda i, j: (l, j, 0)),
        ],
        out_specs=pl.BlockSpec((tm, D), lambda i, j: (i, 0)),
        out_shape=jax.ShapeDtypeStruct((M, D), F32),
        scratch_shapes=[pltpu.VMEM((tm, D), BF16)],
        compiler_params=_params(("parallel", "arbitrary")),
        name="ffn",
    )(x, g, w_in, w_in, w_out)


def _inproj_kernel(x_ref, g_ref, wa_ref, wb_ref, wif_ref, kin_ref, vin_ref,
                   o_ref, gif_ref, k_ref, v_ref, h_ref, *, nG, nA, nQ):
    del kin_ref, vin_ref
    j = pl.program_id(1)

    @pl.when(j == 0)
    def _():
        h32 = _rms(x_ref[...], g_ref[...])
        hb = h32.astype(BF16)
        h_ref[...] = hb
        hl = (h32 - hb.astype(F32)).astype(BF16)
        gif_ref[...] = _dot(hb, wif_ref[0]) + _dot(hb, wif_ref[1]) + _dot(hl, wif_ref[0])

    jA, jQ = nG, nG + nA
    jK, jV = jQ + nQ, jQ + 2 * nQ

    @pl.when((j >= jA) & (j < jQ))
    def _():
        o_ref[...] = _dot(h_ref[...], wa_ref[...])

    @pl.when((j < jA) | ((j >= jQ) & (j < jK)))
    def _():
        o_ref[...] = _dot(h_ref[...], wb_ref[...])

    @pl.when((j >= jK) & (j < jV))
    def _():
        k_ref[...] = _dot(h_ref[...], wb_ref[...])

    @pl.when(j >= jV)
    def _():
        v_ref[...] = _dot(h_ref[...], wb_ref[...])


def _inproj(x, g, w_a, w_b, w_if, k_stack, v_stack, l, tm, tn):
    M, D = x.shape
    nG, nA, nQ = 2 * D // tn, (QK_WIDTH + 2 * M_WIDTH) // tn, A_WIDTH // tn
    NP = 2 * D + QK_WIDTH + 2 * M_WIDTH + A_WIDTH
    nj = nG + nA + 3 * nQ
    kern = functools.partial(_inproj_kernel, nG=nG, nA=nA, nQ=nQ)
    a_idx = lambda j: jnp.clip(j - nG, 0, nA - 1)
    b_idx = lambda j: jnp.where(j < nG, j + 3 * nQ, jnp.clip(j - nG - nA, 0, 3 * nQ - 1))
    return pl.pallas_call(
        kern,
        grid=(M // tm, nj),
        in_specs=[
            pl.BlockSpec((tm, D), lambda i, j: (i, 0)),
            pl.BlockSpec((None, 1, D), lambda i, j: (l, 0, 0)),
            pl.BlockSpec((None, D, tn), lambda i, j: (l, 0, a_idx(j))),
            pl.BlockSpec((None, D, tn), lambda i, j: (l, 0, b_idx(j))),
            pl.BlockSpec((None, 2, D, LANE), lambda i, j: (l, 0, 0, 0)),
            pl.BlockSpec(memory_space=pl.ANY),
            pl.BlockSpec(memory_space=pl.ANY),
        ],
        out_specs=[
            pl.BlockSpec((tm, tn), lambda i, j: (i, jnp.minimum(j, nG + nA + nQ - 1))),
            pl.BlockSpec((tm, LANE), lambda i, j: (i, 0)),
            pl.BlockSpec((None, tm, tn), lambda i, j: (l, i, jnp.clip(j - (nG + nA + nQ), 0, nQ - 1))),
            pl.BlockSpec((None, tm, tn), lambda i, j: (l, i, jnp.clip(j - (nG + nA + 2 * nQ), 0, nQ - 1))),
        ],
        out_shape=[jax.ShapeDtypeStruct((M, NP), F32), jax.ShapeDtypeStruct((M, LANE), F32),
                   jax.ShapeDtypeStruct(k_stack.shape, F32), jax.ShapeDtypeStruct(v_stack.shape, F32)],
        scratch_shapes=[pltpu.VMEM((tm, D), BF16)],
        input_output_aliases={5: 2, 6: 3},
        compiler_params=_params(("parallel", "arbitrary")),
        name="inproj",
    )(x, g, w_a, w_b, w_if, k_stack, v_stack)


def _mlstm_kernel(qp_ref, kp_ref, v_ref, og_ref, gif_ref, bif_ref, cw_ref, cb_ref, mn_ref,
                  cbuf_ref, C0_ref, n0_ref, m0_ref,
                  hm_ref, C_ref, n_ref, m_ref, xq_ref, xk_ref, *, L, mm_dtype):
    c = pl.program_id(1)
    half = QK_WIDTH // 2

    @pl.when(c == 0)
    def _():
        xq_ref[0:CONV_PAD, :] = cbuf_ref[:, 0:half]
        xk_ref[0:CONV_PAD, :] = cbuf_ref[:, half:QK_WIDTH]
        C_ref[...] = C0_ref[...]
        n_ref[...] = n0_ref[...]
        m_ref[...] = m0_ref[...]

    @pl.when(c > 0)
    def _():
        xq_ref[0:CONV_PAD, :] = xq_ref[L:L + CONV_PAD, :]
        xk_ref[0:CONV_PAD, :] = xk_ref[L:L + CONV_PAD, :]

    xq_ref[CONV_PAD:CONV_PAD + L, :] = qp_ref[...]
    xk_ref[CONV_PAD:CONV_PAD + L, :] = kp_ref[...]

    cw = cw_ref[...]
    cb = cb_ref[...]

    def conv_silu(x_ref, w, b):
        base = CONV_PAD - (CONV_W - 1)
        y = b + x_ref[pl.ds(base, L), :] * w[0:1, :]
        for i in range(1, CONV_W):
            y = y + x_ref[pl.ds(base + i, L), :] * w[i:i + 1, :]
        return y * jax.nn.sigmoid(y)

    qa = conv_silu(xq_ref, cw[:, 0:half], cb[:, 0:half])
    ka = conv_silu(xk_ref, cw[:, half:QK_WIDTH], cb[:, half:QK_WIDTH]) * (M_DK ** -0.5)
    g = gif_ref[...] + bif_ref[...]

    ti = lax.broadcasted_iota(jnp.int32, (L, L), 0)
    si = lax.broadcasted_iota(jnp.int32, (L, L), 1)
    eye = ti == si
    low = si <= ti
    upp = ti <= si

    for h in range(M_HEADS):
        sl = slice(h * M_DV, (h + 1) * M_DV)
        ig_col = g[:, h:h + 1]
        gf = g[:, M_HEADS + h:M_HEADS + h + 1]
        lf_col = jnp.minimum(gf, 0.0) - jnp.log1p(jnp.exp(-jnp.abs(gf)))
        ig_row = jnp.sum(jnp.where(eye, ig_col, 0.0), axis=0, keepdims=True)
        lf_row = jnp.sum(jnp.where(eye, lf_col, 0.0), axis=0, keepdims=True)
        F_col = jnp.sum(jnp.where(low, lf_row, 0.0), axis=1, keepdims=True)
        F_row = jnp.sum(jnp.where(upp, lf_col, 0.0), axis=0, keepdims=True)
        m0 = m_ref[h]
        C0 = C_ref[h]
        n0 = n_ref[h]
        logD = jnp.where(low, F_col - F_row + ig_row, -jnp.inf)
        lst = F_col + m0
        m_col = jnp.maximum(lst, jnp.max(logD, axis=1, keepdims=True))
        q = qa[:, sl]
        k = ka[:, sl]
        qb = q.astype(mm_dtype)
        kb = k.astype(mm_dtype)
        vb = v_ref[:, sl].astype(mm_dtype)
        wgt = jnp.exp(logD - m_col) * _dot_nt(qb, kb)
        sc = jnp.exp(lst - m_col)
        num = _dot(wgt.astype(mm_dtype), vb) + sc * _dot(qb, C0.astype(mm_dtype))
        den = jnp.sum(wgt, axis=1, keepdims=True) + sc * jnp.sum(q * n0, axis=1, keepdims=True)
        hh = num / jnp.maximum(jnp.abs(den), jnp.exp(-m_col))
        FT = F_col[L - 1:L, :]
        lT = FT + m0
        m_new = jnp.maximum(lT, jnp.max(FT - F_row + ig_row, axis=1, keepdims=True))
        e_col = jnp.exp(FT - F_col + ig_col - m_new)
        sT = jnp.exp(lT - m_new)
        ke = k * e_col
        C_ref[h] = sT * C0 + _dot_tn(ke.astype(mm_dtype), vb)
        n_ref[h] = sT * n0 + jnp.sum(ke, axis=0, keepdims=True)
        m_ref[h] = m_new
        hn = hh * lax.rsqrt(jnp.mean(hh * hh, axis=1, keepdims=True) + EPS)
        out = jax.nn.sigmoid(og_ref[:, sl]) * hn * mn_ref[:, sl]
        hm_ref[:, sl] = out.astype(hm_ref.dtype)


def _mlstm(proj, gif, b_if, conv_w, conv_b, m_norm, cbuf, C0, n0, m0, l, ls, B, T, col0, out_dtype, mm_dtype):
    L = CHUNK if T % CHUNK == 0 else T
    nc = T // L
    qo = col0 // M_WIDTH
    kern = functools.partial(_mlstm_kernel, L=L, mm_dtype=mm_dtype)
    row = lambda b, c: b * nc + c
    return pl.pallas_call(
        kern,
        grid=(B, nc),
        in_specs=[
            pl.BlockSpec((L, M_WIDTH), lambda b, c: (row(b, c), qo)),
            pl.BlockSpec((L, M_WIDTH), lambda b, c: (row(b, c), qo + 1)),
            pl.BlockSpec((L, M_WIDTH), lambda b, c: (row(b, c), qo + 2)),
            pl.BlockSpec((L, M_WIDTH), lambda b, c: (row(b, c), qo + 3)),
            pl.BlockSpec((L, LANE), lambda b, c: (row(b, c), 0)),
            pl.BlockSpec((None, 1, LANE), lambda b, c: (l, 0, 0)),
            pl.BlockSpec((None, CONV_W, QK_WIDTH), lambda b, c: (l, 0, 0)),
            pl.BlockSpec((None, 1, QK_WIDTH), lambda b, c: (l, 0, 0)),
            pl.BlockSpec((None, 1, M_WIDTH), lambda b, c: (l, 0, 0)),
            pl.BlockSpec((None, None, CONV_PAD, QK_WIDTH), lambda b, c: (ls, b, 0, 0)),
            pl.BlockSpec((None, None, M_HEADS, M_DK, M_DV), lambda b, c: (ls, b, 0, 0, 0)),
            pl.BlockSpec((None, None, M_HEADS, 1, M_DK), lambda b, c: (ls, b, 0, 0, 0)),
            pl.BlockSpec((None, None, M_HEADS, 1, 1), lambda b, c: (ls, b, 0, 0, 0)),
        ],
        out_specs=[
            pl.BlockSpec((L, M_WIDTH), lambda b, c: (row(b, c), 0)),
            pl.BlockSpec((None, M_HEADS, M_DK, M_DV), lambda b, c: (b, 0, 0, 0)),
            pl.BlockSpec((None, M_HEADS, 1, M_DK), lambda b, c: (b, 0, 0, 0)),
            pl.BlockSpec((None, M_HEADS, 1, 1), lambda b, c: (b, 0, 0, 0)),
        ],
        out_shape=[
            jax.ShapeDtypeStruct((B * T, M_WIDTH), out_dtype),
            jax.ShapeDtypeStruct((B, M_HEADS, M_DK, M_DV), F32),
            jax.ShapeDtypeStruct((B, M_HEADS, 1, M_DK), F32),
            jax.ShapeDtypeStruct((B, M_HEADS, 1, 1), F32),
        ],
        scratch_shapes=[pltpu.VMEM((L + CONV_PAD, M_WIDTH), F32), pltpu.VMEM((L + CONV_PAD, M_WIDTH), F32)],
        compiler_params=_params(("parallel", "arbitrary")),
        name="mlstm",
    )(proj, proj, proj, proj, gif, b_if, conv_w, conv_b, m_norm, cbuf, C0, n0, m0)


STRIDE = 4


def _attn_kernel(q_ref, k_ref, v_ref, bias_ref, o_ref,
                 p4, p16, qbs, kbs, vbs, ops, lps, o4, l4, on, ln, *, S):
    scale = A_DH ** -0.5
    QB = Q_BLOCK
    nunits = S // QB
    G = S // STRIDE
    srcs = (q_ref, k_ref, v_ref)

    def load_operands(p, get):
        qb, kb, vb = qbs.at[p], kbs.at[p], vbs.at[p]
        kb[0:QB, :] = jnp.zeros((QB, A_DH), BF16)
        vb[0:QB, 0:A_DH] = jnp.zeros((QB, A_DH), BF16)
        vb[:, A_DH:2 * A_DH] = jnp.ones((S + QB, A_DH), BF16)
        qb[...] = (get(0) * scale).astype(BF16)
        kb[QB:QB + S, :] = get(1).astype(BF16)
        vb[QB:QB + S, 0:A_DH] = get(2).astype(BF16)

    def unit(u, p, nblk, dst_o, dst_l):
        qb, kb, vb = qbs.at[p], kbs.at[p], vbs.at[p]
        r0 = u * QB
        qv = qb[r0:r0 + QB, :]
        if nblk == 1:
            kw = kb[QB + r0:2 * QB + r0, :]
            vw = vb[QB + r0:2 * QB + r0, :]
            bias = bias_ref[p, 0, :, QB:2 * QB]
        else:
            kw = kb[r0:r0 + 2 * QB, :]
            vw = vb[r0:r0 + 2 * QB, :]
            bias = bias_ref[p, 1 if u % nblk == 0 else 0]
        s = _dot_nt(qv, kw) + bias
        m = jnp.max(s, axis=1, keepdims=True)
        pv = _dot(jnp.exp(s - m).astype(BF16), vw)
        den = pv[:, A_DH:2 * A_DH]
        dst_o[r0:r0 + QB, :] = pv[:, 0:A_DH] / den
        dst_l[r0:r0 + QB, :] = m + jnp.log(den)

    load_operands(0, lambda a: srcs[a][...])
    for u in range(nunits):
        unit(u, 0, nunits, on.at[0], ln.at[0])

    for a in range(3):
        for r in range(STRIDE):
            p4[a, r * G:(r + 1) * G, :] = srcs[a][pl.ds(r, G, stride=STRIDE), :]
    load_operands(1, lambda a: p4[a])
    for u in range(nunits):
        unit(u, 1, G // QB, ops.at[0], lps.at[0])
    for r in range(STRIDE):
        on[1, pl.ds(r, G, stride=STRIDE), :] = ops[0, r * G:(r + 1) * G, :]
        ln[1, pl.ds(r, G, stride=STRIDE), :] = lps[0, r * G:(r + 1) * G, :]

    for a in range(3):
        for r in range(STRIDE):
            for c in range(STRIDE):
                u = r * STRIDE + c
                p16[a, u * QB:(u + 1) * QB, :] = p4[a, pl.ds(r * G + c, QB, stride=STRIDE), :]
    load_operands(2, lambda a: p16[a])
    for u in range(nunits):
        unit(u, 2, 1, ops.at[1], lps.at[1])
    for r in range(STRIDE):
        for c in range(STRIDE):
            u = r * STRIDE + c
            o4[pl.ds(r * G + c, QB, stride=STRIDE), :] = ops[1, u * QB:(u + 1) * QB, :]
            l4[pl.ds(r * G + c, QB, stride=STRIDE), :] = lps[1, u * QB:(u + 1) * QB, :]
    for r in range(STRIDE):
        on[2, pl.ds(r, G, stride=STRIDE), :] = o4[r * G:(r + 1) * G, :]
        ln[2, pl.ds(r, G, stride=STRIDE), :] = l4[r * G:(r + 1) * G, :]

    def combine(bi, carry):
        rows = pl.ds(pl.multiple_of(bi * QB, QB), QB)
        ls = [ln[p, rows, :] for p in range(len(PATTERNS))]
        lmax = functools.reduce(jnp.maximum, ls)
        tot = jnp.zeros((QB, A_DH), F32)
        wsum = jnp.zeros((QB, A_DH), F32)
        for p in range(len(PATTERNS)):
            a = jnp.exp(ls[p] - lmax)
            tot = tot + a * on[p, rows, :]
            wsum = wsum + a
        o_ref[rows, :] = (tot / wsum).astype(o_ref.dtype)
        return carry

    lax.fori_loop(0, nunits, combine, 0)


def _attn_prompt(proj, k_stack, v_stack, bias, l, B, S, qcol):
    assert S == N_STEPS * PATTERNS[-1][1] and PATTERNS[1][1] == STRIDE and PATTERNS[2][1] == STRIDE * STRIDE
    H = A_HEADS
    cb = qcol // A_DH
    kern = functools.partial(_attn_kernel, S=S)
    npat = len(PATTERNS)
    return pl.pallas_call(
        kern,
        grid=(B, H),
        in_specs=[
            pl.BlockSpec((S, A_DH), lambda b, h: (b, cb + h)),
            pl.BlockSpec((None, S, A_DH), lambda b, h: (l, b, h)),
            pl.BlockSpec((None, S, A_DH), lambda b, h: (l, b, h)),
            pl.BlockSpec((None, npat, 2, Q_BLOCK, 2 * Q_BLOCK), lambda b, h: (h, 0, 0, 0, 0)),
        ],
        out_specs=pl.BlockSpec((S, A_DH), lambda b, h: (b, h)),
        out_shape=jax.ShapeDtypeStruct((B * S, A_WIDTH), BF16),
        scratch_shapes=[
            pltpu.VMEM((3, S, A_DH), F32),
            pltpu.VMEM((3, S, A_DH), F32),
            pltpu.VMEM((npat, S, A_DH), BF16),
            pltpu.VMEM((npat, S + Q_BLOCK, A_DH), BF16),
            pltpu.VMEM((npat, S + Q_BLOCK, 2 * A_DH), BF16),
            pltpu.VMEM((2, S, A_DH), F32),
            pltpu.VMEM((2, S, A_DH), F32),
            pltpu.VMEM((S, A_DH), F32),
            pltpu.VMEM((S, A_DH), F32),
            pltpu.VMEM((npat, S, A_DH), F32),
            pltpu.VMEM((npat, S, A_DH), F32),
        ],
        compiler_params=_params(("parallel", "parallel")),
        name="attn_prompt",
    )(proj, k_stack, v_stack, bias)


def _attn_s_kernel(q_ref, kn_ref, vn_ref, kc_ref, vc_ref, bc_ref, bn_ref, o_ref, *, Lbuf):
    H = A_HEADS
    npat = len(PATTERNS)
    for h in range(H):
        hs = slice(h * A_DH, (h + 1) * A_DH)
        q = q_ref[:, hs] * (A_DH ** -0.5)
        kc = kc_ref[pl.ds(h, Lbuf, stride=H), :]
        vc = vc_ref[pl.ds(h, Lbuf, stride=H), :]
        s_c = _dot_nt(q, kc)
        s_n = _dot_nt(q, kn_ref[:, hs])
        mx = None
        for p in range(npat):
            mp = jnp.maximum(jnp.max(s_c + bc_ref[h, p], axis=1, keepdims=True),
                             jnp.max(s_n + bn_ref[h, p], axis=1, keepdims=True))
            mx = mp if mx is None else jnp.maximum(mx, mp)
        e_c = jnp.exp(s_c + bc_ref[h, 0] - mx)
        e_n = jnp.exp(s_n + bn_ref[h, 0] - mx)
        for p in range(1, npat):
            e_c = e_c + jnp.exp(s_c + bc_ref[h, p] - mx)
            e_n = e_n + jnp.exp(s_n + bn_ref[h, p] - mx)
        num = _dot(e_c, vc) + _dot(e_n, vn_ref[:, hs])
        den = jnp.sum(e_c, axis=1, keepdims=True) + jnp.sum(e_n, axis=1, keepdims=True)
        o_ref[:, hs] = num / den


def _attn_sample(proj, k_stack, v_stack, cache_k, cache_v, bias_c, bias_n, l, B, T, qcol):
    H = A_HEADS
    Lbuf = cache_k.shape[2] // H
    npat = len(PATTERNS)
    kern = functools.partial(_attn_s_kernel, Lbuf=Lbuf)
    return pl.pallas_call(
        kern,
        grid=(B,),
        in_specs=[
            pl.BlockSpec((T, A_WIDTH), lambda b: (b, qcol // A_WIDTH)),
            pl.BlockSpec((None, T, A_WIDTH), lambda b: (l, b, 0)),
            pl.BlockSpec((None, T, A_WIDTH), lambda b: (l, b, 0)),
            pl.BlockSpec((None, None, Lbuf * H, A_DH), lambda b: (l, b, 0, 0)),
            pl.BlockSpec((None, None, Lbuf * H, A_DH), lambda b: (l, b, 0, 0)),
            pl.BlockSpec((H, npat, T, Lbuf), lambda b: (0, 0, 0, 0)),
            pl.BlockSpec((H, npat, T, T), lambda b: (0, 0, 0, 0)),
        ],
        out_specs=pl.BlockSpec((T, A_WIDTH), lambda b: (b, 0)),
        out_shape=jax.ShapeDtypeStruct((B * T, A_WIDTH), F32),
        compiler_params=_params(("parallel",)),
        name="attn_sample",
    )(proj, k_stack, v_stack, cache_k, cache_v, bias_c, bias_n)


def _outproj_kernel(x_ref, hm_ref, ha_ref, gm_ref, ga_ref, wpm_ref, wpa_ref, wo_ref, o_ref):
    pm = _dot(hm_ref[...].astype(BF16), wpm_ref[...])
    pa = _dot(ha_ref[...].astype(BF16), wpa_ref[...])
    merged = jax.nn.sigmoid(gm_ref[...]) * pm + jax.nn.sigmoid(ga_ref[...]) * pa
    o_ref[...] = x_ref[...] + _dot(merged.astype(BF16), wo_ref[...])


def _outproj(x, hm, ha, proj, w_pm, w_pa, w_out, l, tm):
    M, D = x.shape
    once = pl.Buffered(1)
    return pl.pallas_call(
        _outproj_kernel,
        grid=(M // tm,),
        in_specs=[
            pl.BlockSpec((tm, D), lambda i: (i, 0)),
            pl.BlockSpec((tm, M_WIDTH), lambda i: (i, 0)),
            pl.BlockSpec((tm, A_WIDTH), lambda i: (i, 0)),
            pl.BlockSpec((tm, D), lambda i: (i, 0)),
            pl.BlockSpec((tm, D), lambda i: (i, 1)),
            pl.BlockSpec((None, M_WIDTH, D), lambda i: (l, 0, 0), pipeline_mode=once),
            pl.BlockSpec((None, A_WIDTH, D), lambda i: (l, 0, 0), pipeline_mode=once),
            pl.BlockSpec((None, D, D), lambda i: (l, 0, 0), pipeline_mode=once),
        ],
        out_specs=pl.BlockSpec((tm, D), lambda i: (i, 0)),
        out_shape=jax.ShapeDtypeStruct((M, D), F32),
        compiler_params=_params(("parallel",)),
        name="outproj",
    )(x, hm, ha, proj, proj, w_pm, w_pa, w_out)


def _norm_kernel(x_ref, g_ref, o_ref):
    o_ref[...] = _rms(x_ref[...], g_ref[...])


def _final_norm(x, g, tm):
    M, D = x.shape
    return pl.pallas_call(
        _norm_kernel,
        grid=(M // tm,),
        in_specs=[pl.BlockSpec((tm, D), lambda i: (i, 0)), pl.BlockSpec((1, D), lambda i: (0, 0))],
        out_specs=pl.BlockSpec((tm, D), lambda i: (i, 0)),
        out_shape=jax.ShapeDtypeStruct((M, D), F32),
        compiler_params=_params(("parallel",)),
        name="final_norm",
    )(x, g)


def _t5_bucket(dist):
    exact = N_BUCKETS // 2
    d32 = jnp.maximum(dist, 1).astype(F32)
    large = exact + (jnp.log(d32 / exact) / math.log(MAX_DISTANCE / exact) * (N_BUCKETS - exact)).astype(jnp.int32)
    large = jnp.minimum(large, N_BUCKETS - 1)
    return jnp.where(dist < exact, dist, large)


def _step_biases(rel_table):
    assert all(w // d == N_STEPS for w, d in PATTERNS)
    buckets = jnp.stack([_t5_bucket(d * jnp.arange(N_STEPS + 1, dtype=jnp.int32)) for (w, d) in PATTERNS])
    onehot = (buckets[:, :, None] == jnp.arange(N_BUCKETS)[None, None, :]).astype(F32)
    return jnp.einsum("pjn,nh->phj", onehot, rel_table.astype(F32), precision=lax.Precision.HIGHEST)


def _toeplitz(g, rows, cols, off):
    n = g.shape[-1]
    width = off + cols
    assert width <= n - 1 and off - (rows - 1) >= 0
    flat = jnp.tile(g, (1,) * (g.ndim - 1) + (rows,))[..., :rows * (n - 1)]
    return flat.reshape(g.shape[:-1] + (rows, n - 1))[..., off:off + cols]


def _prompt_bias(sb):
    Q = Q_BLOCK
    ninf = lambda n: jnp.full(sb.shape[:-1] + (n,), -jnp.inf, F32)
    g = jnp.concatenate([ninf(Q - 1), sb[..., ::-1], ninf(Q)], axis=-1)
    base = _toeplitz(g, Q, 2 * Q, Q - 1)
    first = jnp.where(jnp.arange(2 * Q) >= Q, base, -jnp.inf)
    return jnp.transpose(jnp.stack([base, first], axis=2), (1, 0, 2, 3, 4))


def _sample_bias(sb, T, Lbuf):
    n = Lbuf + T
    tabs = []
    for p, (w, d) in enumerate(PATTERNS):
        bd = jnp.concatenate([sb[p][..., None], jnp.full(sb[p].shape + (d - 1,), -jnp.inf, F32)], axis=-1)
        bd = bd.reshape(sb.shape[1], -1)
        bd = jnp.concatenate([bd, jnp.full((sb.shape[1], n), -jnp.inf, F32)], axis=-1)[:, :n]
        g = jnp.concatenate([bd[:, ::-1], jnp.full((sb.shape[1], T), -jnp.inf, F32)], axis=-1)
        tabs.append(_toeplitz(g, T, n, T - 1))
    tab = jnp.stack(tabs, axis=1)
    return tab[..., :Lbuf], tab[..., Lbuf:]


def _row_tile(M, cap):
    return cap if M % cap == 0 else M


def _trunk(x3, wts, sb, states):
    B, T, D = x3.shape
    M = B * T
    depth = wts["w_a"].shape[0]
    prompt = states is None
    FF = wts["w_ffa_out"].shape[1]
    tm = _row_tile(M, 1024)
    tf = 512 if FF % 512 == 0 else FF
    tn = 512
    tm_out = _row_tile(M, 512)
    col_m = 2 * D
    col_q = 2 * D + QK_WIDTH + 2 * M_WIDTH
    if prompt:
        bias = _prompt_bias(sb)
        cbuf = jnp.zeros((1, B, CONV_PAD, QK_WIDTH), F32)
        C0 = jnp.zeros((1, B, M_HEADS, M_DK, M_DV), F32)
        n0 = jnp.zeros((1, B, M_HEADS, 1, M_DK), F32)
        m0 = jnp.zeros((1, B, M_HEADS, 1, 1), F32)
    else:
        cache_k, cache_v, sC, sn, sm, sconv = states
        Lbuf = cache_k.shape[2]
        bias_c, bias_n = _sample_bias(sb, T, Lbuf)
        cache_k = cache_k.astype(F32).reshape(depth, B, Lbuf * A_HEADS, A_DH)
        cache_v = cache_v.astype(F32).reshape(depth, B, Lbuf * A_HEADS, A_DH)
        cbuf = jnp.pad(sconv.astype(F32), ((0, 0), (0, 0), (CONV_PAD - (CONV_W - 1), 0), (0, 0)))
        C0 = sC.astype(F32)
        n0 = sn.astype(F32).reshape(depth, B, M_HEADS, 1, M_DK)
        m0 = sm.astype(F32).reshape(depth, B, M_HEADS, 1, 1)

    x = x3.reshape(M, D)
    k_stack = jnp.zeros((depth, M, A_WIDTH), F32)
    v_stack = jnp.zeros((depth, M, A_WIDTH), F32)
    Cs, ns, ms, convs = [], [], [], []
    for l in range(depth):
        x = _ffn(x, wts["ln_ffa"], wts["w_ffa_in"], wts["w_ffa_out"], l, tm, tf)
        proj, gif, k_stack, v_stack = _inproj(x, wts["ln_mix"], wts["w_a"], wts["w_b"], wts["w_if"],
                                              k_stack, v_stack, l, tm, tn)
        ls = 0 if prompt else l
        hm, C1, n1, m1 = _mlstm(proj, gif, wts["b_if"], wts["conv_w"], wts["conv_b"], wts["m_norm"],
                                cbuf, C0, n0, m0, l, ls, B, T, col_m,
                                BF16 if prompt else F32, BF16 if prompt else F32)
        if prompt:
            ha = _attn_prompt(proj, k_stack, v_stack, bias, l, B, T, col_q)
        else:
            ha = _attn_sample(proj, k_stack, v_stack, cache_k, cache_v, bias_c, bias_n, l, B, T, col_q)
        x = _outproj(x, hm, ha, proj, wts["w_pm"], wts["w_pa"], wts["w_out"], l, tm_out)
        x = _ffn(x, wts["ln_ffb"], wts["w_ffb_in"], wts["w_ffb_out"], l, tm, tf)
        Cs.append(C1)
        ns.append(n1.reshape(B, M_HEADS, M_DK))
        ms.append(m1.reshape(B, M_HEADS))
        pre = proj.reshape(B, T, -1)[:, :, col_m:col_m + QK_WIDTH]
        if T >= CONV_W - 1:
            convs.append(pre[:, T - (CONV_W - 1):])
        else:
            convs.append(jnp.concatenate([cbuf[ls][:, CONV_PAD - (CONV_W - 1) + T:], pre], axis=1))
    y = _final_norm(x, wts["ln_f"], tm).reshape(B, T, D)
    stk = lambda xs: jnp.stack(xs, axis=0)
    keep = min(MAX_WINDOW, T)
    k_new = k_stack.reshape(depth, B, T, A_HEADS, A_DH)[:, :, T - keep:]
    v_new = v_stack.reshape(depth, B, T, A_HEADS, A_DH)[:, :, T - keep:]
    return y, (k_new, v_new, stk(Cs), stk(ns), stk(ms), stk(convs))


def kernel(x_prompt, x_sample, cache_k, cache_v, state_C, state_n, state_m, state_conv, w_in, conv_w, conv_b, b_if, m_norm, w_pm, w_pa, w_out, rel_table, ln_ffa, w_ffa_in, w_ffa_out, ln_mix, ln_ffb, w_ffb_in, w_ffb_out, ln_f):
    depth, D, _ = w_in.shape
    o_if = QK_WIDTH + 2 * M_WIDTH
    o_a = o_if + 2 * M_HEADS
    w_if32 = jnp.pad(w_in[:, :, o_if:o_a], ((0, 0), (0, 0), (0, LANE - 2 * M_HEADS)))
    w_if_hi = w_if32.astype(BF16)
    w_if_lo = (w_if32 - w_if_hi.astype(F32)).astype(BF16)
    wts = dict(
        w_a=w_in[:, :, :o_if].astype(BF16),
        w_b=w_in[:, :, o_a:].astype(BF16),
        w_if=jnp.stack([w_if_hi, w_if_lo], axis=1),
        b_if=jnp.pad(b_if.astype(F32), ((0, 0), (0, LANE - 2 * M_HEADS))).reshape(depth, 1, LANE),
        conv_w=conv_w.astype(F32),
        conv_b=conv_b.astype(F32).reshape(depth, 1, QK_WIDTH),
        m_norm=m_norm.astype(F32).reshape(depth, 1, M_WIDTH),
        w_pm=w_pm.astype(BF16), w_pa=w_pa.astype(BF16), w_out=w_out.astype(BF16),
        ln_ffa=ln_ffa.astype(F32).reshape(depth, 1, D), ln_mix=ln_mix.astype(F32).reshape(depth, 1, D),
        ln_ffb=ln_ffb.astype(F32).reshape(depth, 1, D), ln_f=ln_f.astype(F32).reshape(1, D),
        w_ffa_in=w_ffa_in.astype(BF16), w_ffa_out=w_ffa_out.astype(BF16),
        w_ffb_in=w_ffb_in.astype(BF16), w_ffb_out=w_ffb_out.astype(BF16),
    )
    sb = _step_biases(rel_table)
    y_p, (k_p, v_p, C_p, n_p, m_p, conv_p) = _trunk(x_prompt, wts, sb, None)
    y_s, (k_s, v_s, C_s, n_s, m_s, conv_s) = _trunk(
        x_sample, wts, sb, (cache_k, cache_v, state_C, state_n, state_m, state_conv))
    return (y_p, y_s, k_p, v_p, C_p, n_p, m_p, conv_p, k_s, v_s, C_s, n_s, m_s, conv_s)
```

```python
import functools
import math

import jax
import jax.numpy as jnp
from jax import lax
from jax.experimental import pallas as pl
from jax.experimental.pallas import tpu as pltpu

F32 = jnp.float32
BF16 = jnp.bfloat16

M_HEADS = 4
M_DK = 256
M_DV = 256
M_WIDTH = M_HEADS * M_DV
QK_WIDTH = 2 * M_HEADS * M_DK
CONV_W = 4
CHUNK = 128
A_HEADS = 8
A_DH = 128
A_WIDTH = A_HEADS * A_DH
PATTERNS = ((128, 1), (512, 4), (2048, 16))
MAX_WINDOW = 2048
Q_BLOCK = 128
N_BUCKETS = 32
MAX_DISTANCE = MAX_WINDOW
EPS = 1e-6
N_STEPS = 128
LANE = 128
CONV_PAD = 8
VMEM_LIMIT = 58 * 1024 * 1024


def _dot(a, b):
    return jnp.dot(a, b, preferred_element_type=F32)


def _dot_nt(a, b):
    return lax.dot_general(a, b, (((1,), (1,)), ((), ())), preferred_element_type=F32)


def _dot_tn(a, b):
    return lax.dot_general(a, b, (((0,), (0,)), ((), ())), preferred_element_type=F32)


def _rms(x, g):
    return x * lax.rsqrt(jnp.mean(x * x, axis=-1, keepdims=True) + EPS) * g


def _params(sem):
    return pltpu.CompilerParams(dimension_semantics=sem, vmem_limit_bytes=VMEM_LIMIT)


def _ffn_kernel(x_ref, g_ref, wg_ref, wu_ref, wo_ref, o_ref, h_ref):
    j = pl.program_id(1)

    @pl.when(j == 0)
    def _():
        h_ref[...] = _rms(x_ref[...], g_ref[...]).astype(BF16)
        o_ref[...] = jnp.zeros_like(o_ref)

    h = h_ref[...]
    gate = _dot(h, wg_ref[...])
    up = _dot(h, wu_ref[...])
    act = (gate * jax.nn.sigmoid(gate) * up).astype(BF16)
    o_ref[...] += _dot(act, wo_ref[...])

    @pl.when(j == pl.num_programs(1) - 1)
    def _():
        o_ref[...] = x_ref[...] + 0.5 * o_ref[...]


def _ffn(x, g, w_in, w_out, l, tm, tf):
    M, D = x.shape
    FF = w_out.shape[1]
    nj = FF // tf
    return pl.pallas_call(
        _ffn_kernel,
        grid=(M // tm, nj),
        in_specs=[
            pl.BlockSpec((tm, D), lambda i, j: (i, 0)),
            pl.BlockSpec((None, 1, D), lambda i, j: (l, 0, 0)),
            pl.BlockSpec((None, D, tf), lambda i, j: (l, 0, j)),
            pl.BlockSpec((None, D, tf), lambda i, j: (l, 0, j + nj)),
            pl.BlockSpec((None, tf, D), lambda i, j: (l, j, 0)),
        ],
        out_specs=pl.BlockSpec((tm, D), lambda i, j: (i, 0)),
        out_shape=jax.ShapeDtypeStruct((M, D), F32),
        scratch_shapes=[pltpu.VMEM((tm, D), BF16)],
        compiler_params=_params(("parallel", "arbitrary")),
        name="ffn",
    )(x, g, w_in, w_in, w_out)


W_TILE = 1024
W_ROWS = 256


def _wprep_kernel(w1_ref, w2_ref, o_ref, *, nA, shift):
    j = pl.program_id(1)
    K, tn = o_ref.shape

    @pl.when(j < nA)
    def _():
        o_ref[...] = w1_ref[...].astype(BF16)

    @pl.when(j >= nA)
    def _():
        for r0 in range(0, K, W_ROWS):
            rows = slice(r0, min(r0 + W_ROWS, K))
            wide = jnp.concatenate([w1_ref[rows, :], w2_ref[rows, :]], axis=1)
            o_ref[rows, :] = wide[:, shift:shift + tn].astype(BF16)


def _wprep(w_in, n_a, shift, tn):
    depth, D, n_in = w_in.shape
    assert n_a % tn == 0 and (n_in - shift) % tn == 0 and shift <= LANE
    nj = (n_in - shift) // tn
    kern = functools.partial(_wprep_kernel, nA=n_a // tn, shift=shift)
    return pl.pallas_call(
        kern,
        grid=(depth, nj),
        in_specs=[
            pl.BlockSpec((None, D, tn), lambda l, j: (l, 0, j)),
            pl.BlockSpec((None, D, LANE), lambda l, j: (l, 0, (j + 1) * (tn // LANE))),
        ],
        out_specs=pl.BlockSpec((None, D, tn), lambda l, j: (l, 0, j)),
        out_shape=jax.ShapeDtypeStruct((depth, D, n_in - shift), BF16),
        compiler_params=_params(("parallel", "parallel")),
        name="wprep",
    )(w_in, w_in)


def _inproj_kernel(x_ref, g_ref, w_ref, wif_ref, o_ref, gif_ref, h_ref):
    j = pl.program_id(1)

    @pl.when(j == 0)
    def _():
        h32 = _rms(x_ref[...], g_ref[...])
        hb = h32.astype(BF16)
        h_ref[...] = hb
        hl = (h32 - hb.astype(F32)).astype(BF16)
        gif_ref[...] = _dot(hb, wif_ref[0]) + _dot(hb, wif_ref[1]) + _dot(hl, wif_ref[0])

    o_ref[...] = _dot(h_ref[...], w_ref[...])


def _inproj(x, g, w_ab, w_if, l, tm, tn):
    M, D = x.shape
    nG, nA, nQ = 2 * D // tn, (QK_WIDTH + 2 * M_WIDTH) // tn, A_WIDTH // tn
    NP = 2 * D + QK_WIDTH + 2 * M_WIDTH + A_WIDTH
    w_idx = lambda j: jnp.where(j < nG, j + nA + 3 * nQ, j - nG)
    return pl.pallas_call(
        _inproj_kernel,
        grid=(M // tm, NP // tn),
        in_specs=[
            pl.BlockSpec((tm, D), lambda i, j: (i, 0)),
            pl.BlockSpec((None, 1, D), lambda i, j: (l, 0, 0)),
            pl.BlockSpec((None, D, tn), lambda i, j: (l, 0, w_idx(j))),
            pl.BlockSpec((None, 2, D, LANE), lambda i, j: (l, 0, 0, 0)),
        ],
        out_specs=[
            pl.BlockSpec((tm, tn), lambda i, j: (i, j)),
            pl.BlockSpec((tm, LANE), lambda i, j: (i, 0)),
        ],
        out_shape=[jax.ShapeDtypeStruct((M, NP), F32), jax.ShapeDtypeStruct((M, LANE), F32)],
        scratch_shapes=[pltpu.VMEM((tm, D), BF16)],
        compiler_params=_params(("parallel", "arbitrary")),
        name="inproj",
    )(x, g, w_ab, w_if)


def _kvproj_kernel(x_ref, g_ref, w_ref, kin_ref, vin_ref, k_ref, v_ref, h_ref, *, nQ):
    del kin_ref, vin_ref
    j = pl.program_id(1)

    @pl.when(j == 0)
    def _():
        h_ref[...] = _rms(x_ref[...], g_ref[...]).astype(BF16)

    @pl.when(j < nQ)
    def _():
        k_ref[...] = _dot(h_ref[...], w_ref[...])

    @pl.when(j >= nQ)
    def _():
        v_ref[...] = _dot(h_ref[...], w_ref[...])


def _kvproj(x, g, w_ab, k_stack, v_stack, l, tm, tn):
    M, D = x.shape
    nA, nQ = (QK_WIDTH + 2 * M_WIDTH) // tn, A_WIDTH // tn
    kern = functools.partial(_kvproj_kernel, nQ=nQ)
    return pl.pallas_call(
        kern,
        grid=(M // tm, 2 * nQ),
        in_specs=[
            pl.BlockSpec((tm, D), lambda i, j: (i, 0)),
            pl.BlockSpec((None, 1, D), lambda i, j: (l, 0, 0)),
            pl.BlockSpec((None, D, tn), lambda i, j: (l, 0, nA + nQ + j)),
            pl.BlockSpec(memory_space=pl.ANY),
            pl.BlockSpec(memory_space=pl.ANY),
        ],
        out_specs=[
            pl.BlockSpec((None, tm, tn), lambda i, j: (l, i, jnp.minimum(j, nQ - 1))),
            pl.BlockSpec((None, tm, tn), lambda i, j: (l, i, jnp.maximum(j - nQ, 0))),
        ],
        out_shape=[jax.ShapeDtypeStruct(k_stack.shape, F32), jax.ShapeDtypeStruct(v_stack.shape, F32)],
        scratch_shapes=[pltpu.VMEM((tm, D), BF16)],
        input_output_aliases={3: 0, 4: 1},
        compiler_params=_params(("parallel", "arbitrary")),
        name="kvproj",
    )(x, g, w_ab, k_stack, v_stack)


def _mlstm_kernel(qp_ref, kp_ref, v_ref, og_ref, gif_ref, bif_ref, cw_ref, cb_ref, mn_ref,
                  cbuf_ref, C0_ref, n0_ref, m0_ref,
                  hm_ref, C_ref, n_ref, m_ref, xq_ref, xk_ref, *, L, mm_dtype):
    c = pl.program_id(1)
    half = QK_WIDTH // 2

    @pl.when(c == 0)
    def _():
        xq_ref[0:CONV_PAD, :] = cbuf_ref[:, 0:half]
        xk_ref[0:CONV_PAD, :] = cbuf_ref[:, half:QK_WIDTH]
        C_ref[...] = C0_ref[...]
        n_ref[...] = n0_ref[...]
        m_ref[...] = m0_ref[...]

    @pl.when(c > 0)
    def _():
        xq_ref[0:CONV_PAD, :] = xq_ref[L:L + CONV_PAD, :]
        xk_ref[0:CONV_PAD, :] = xk_ref[L:L + CONV_PAD, :]

    xq_ref[CONV_PAD:CONV_PAD + L, :] = qp_ref[...]
    xk_ref[CONV_PAD:CONV_PAD + L, :] = kp_ref[...]

    cw = cw_ref[...]
    cb = cb_ref[...]

    def conv_silu(x_ref, w, b):
        xs = x_ref[...]
        y = b + xs * w[CONV_W - 1:CONV_W, :]
        for s in range(1, CONV_W):
            y = y + pltpu.roll(xs, s, axis=0) * w[CONV_W - 1 - s:CONV_W - s, :]
        y = y[CONV_PAD:, :]
        return y * jax.nn.sigmoid(y)

    qa = conv_silu(xq_ref, cw[:, 0:half], cb[:, 0:half])
    ka = conv_silu(xk_ref, cw[:, half:QK_WIDTH], cb[:, half:QK_WIDTH]) * (M_DK ** -0.5)
    g = gif_ref[...] + bif_ref[...]

    ti = lax.broadcasted_iota(jnp.int32, (L, L), 0)
    si = lax.broadcasted_iota(jnp.int32, (L, L), 1)
    eye = ti == si
    low = si <= ti
    upp = ti <= si

    for h in range(M_HEADS):
        sl = slice(h * M_DV, (h + 1) * M_DV)
        ig_col = g[:, h:h + 1]
        gf = g[:, M_HEADS + h:M_HEADS + h + 1]
        lf_col = jnp.minimum(gf, 0.0) - jnp.log1p(jnp.exp(-jnp.abs(gf)))
        ig_row = jnp.sum(jnp.where(eye, ig_col, 0.0), axis=0, keepdims=True)
        lf_row = jnp.sum(jnp.where(eye, lf_col, 0.0), axis=0, keepdims=True)
        F_col = jnp.sum(jnp.where(low, lf_row, 0.0), axis=1, keepdims=True)
        F_row = jnp.sum(jnp.where(upp, lf_col, 0.0), axis=0, keepdims=True)
        m0 = m_ref[h]
        C0 = C_ref[h]
        n0 = n_ref[h]
        logD = jnp.where(low, F_col - F_row + ig_row, -jnp.inf)
        lst = F_col + m0
        m_col = jnp.maximum(lst, jnp.max(logD, axis=1, keepdims=True))
        q = qa[:, sl]
        k = ka[:, sl]
        qb = q.astype(mm_dtype)
        kb = k.astype(mm_dtype)
        vb = v_ref[:, sl].astype(mm_dtype)
        wgt = jnp.exp(logD - m_col) * _dot_nt(qb, kb)
        sc = jnp.exp(lst - m_col)
        num = _dot(wgt.astype(mm_dtype), vb) + sc * _dot(qb, C0.astype(mm_dtype))
        den = jnp.sum(wgt, axis=1, keepdims=True) + sc * jnp.sum(q * n0, axis=1, keepdims=True)
        hh = num / jnp.maximum(jnp.abs(den), jnp.exp(-m_col))
        FT = F_col[L - 1:L, :]
        lT = FT + m0
        m_new = jnp.maximum(lT, jnp.max(FT - F_row + ig_row, axis=1, keepdims=True))
        e_col = jnp.exp(FT - F_col + ig_col - m_new)
        sT = jnp.exp(lT - m_new)
        ke = k * e_col
        C_ref[h] = sT * C0 + _dot_tn(ke.astype(mm_dtype), vb)
        n_ref[h] = sT * n0 + jnp.sum(ke, axis=0, keepdims=True)
        m_ref[h] = m_new
        hn = hh * lax.rsqrt(jnp.mean(hh * hh, axis=1, keepdims=True) + EPS)
        out = jax.nn.sigmoid(og_ref[:, sl]) * hn * mn_ref[:, sl]
        hm_ref[:, sl] = out.astype(hm_ref.dtype)


def _mlstm(proj, gif, b_if, conv_w, conv_b, m_norm, cbuf, C0, n0, m0, l, ls, B, T, col0, out_dtype, mm_dtype):
    L = CHUNK if T % CHUNK == 0 else T
    nc = T // L
    qo = col0 // M_WIDTH
    kern = functools.partial(_mlstm_kernel, L=L, mm_dtype=mm_dtype)
    row = lambda b, c: b * nc + c
    return pl.pallas_call(
        kern,
        grid=(B, nc),
        in_specs=[
            pl.BlockSpec((L, M_WIDTH), lambda b, c: (row(b, c), qo)),
            pl.BlockSpec((L, M_WIDTH), lambda b, c: (row(b, c), qo + 1)),
            pl.BlockSpec((L, M_WIDTH), lambda b, c: (row(b, c), qo + 2)),
            pl.BlockSpec((L, M_WIDTH), lambda b, c: (row(b, c), qo + 3)),
            pl.BlockSpec((L, LANE), lambda b, c: (row(b, c), 0)),
            pl.BlockSpec((None, 1, LANE), lambda b, c: (l, 0, 0)),
            pl.BlockSpec((None, CONV_W, QK_WIDTH), lambda b, c: (l, 0, 0)),
            pl.BlockSpec((None, 1, QK_WIDTH), lambda b, c: (l, 0, 0)),
            pl.BlockSpec((None, 1, M_WIDTH), lambda b, c: (l, 0, 0)),
            pl.BlockSpec((None, None, CONV_PAD, QK_WIDTH), lambda b, c: (ls, b, 0, 0)),
            pl.BlockSpec((None, None, M_HEADS, M_DK, M_DV), lambda b, c: (ls, b, 0, 0, 0)),
            pl.BlockSpec((None, None, M_HEADS, 1, M_DK), lambda b, c: (ls, b, 0, 0, 0)),
            pl.BlockSpec((None, None, M_HEADS, 1, 1), lambda b, c: (ls, b, 0, 0, 0)),
        ],
        out_specs=[
            pl.BlockSpec((L, M_WIDTH), lambda b, c: (row(b, c), 0)),
            pl.BlockSpec((None, M_HEADS, M_DK, M_DV), lambda b, c: (b, 0, 0, 0)),
            pl.BlockSpec((None, M_HEADS, 1, M_DK), lambda b, c: (b, 0, 0, 0)),
            pl.BlockSpec((None, M_HEADS, 1, 1), lambda b, c: (b, 0, 0, 0)),
        ],
        out_shape=[
            jax.ShapeDtypeStruct((B * T, M_WIDTH), out_dtype),
            jax.ShapeDtypeStruct((B, M_HEADS, M_DK, M_DV), F32),
            jax.ShapeDtypeStruct((B, M_HEADS, 1, M_DK), F32),
            jax.ShapeDtypeStruct((B, M_HEADS, 1, 1), F32),
        ],
        scratch_shapes=[pltpu.VMEM((L + CONV_PAD, M_WIDTH), F32), pltpu.VMEM((L + CONV_PAD, M_WIDTH), F32)],
        compiler_params=_params(("parallel", "arbitrary")),
        name="mlstm",
    )(proj, proj, proj, proj, gif, b_if, conv_w, conv_b, m_norm, cbuf, C0, n0, m0)


STRIDE = 4


def _attn_kernel(q_ref, k_ref, v_ref, bias_ref, o_ref,
                 p4, p16, qbs, kbs, vbs, ops, lps, o4, l4, on, ln, *, S):
    scale = A_DH ** -0.5
    QB = Q_BLOCK
    nunits = S // QB
    G = S // STRIDE
    srcs = (q_ref, k_ref, v_ref)

    def load_operands(p, get):
        qb, kb, vb = qbs.at[p], kbs.at[p], vbs.at[p]
        kb[0:QB, :] = jnp.zeros((QB, A_DH), BF16)
        vb[0:QB, 0:A_DH] = jnp.zeros((QB, A_DH), BF16)
        vb[:, A_DH:2 * A_DH] = jnp.ones((S + QB, A_DH), BF16)
        qb[...] = (get(0) * scale).astype(BF16)
        kb[QB:QB + S, :] = get(1).astype(BF16)
        vb[QB:QB + S, 0:A_DH] = get(2).astype(BF16)

    def unit(u, p, nblk, dst_o, dst_l):
        qb, kb, vb = qbs.at[p], kbs.at[p], vbs.at[p]
        r0 = u * QB
        qv = qb[r0:r0 + QB, :]
        if nblk == 1:
            kw = kb[QB + r0:2 * QB + r0, :]
            vw = vb[QB + r0:2 * QB + r0, :]
            bias = bias_ref[p, 0, :, QB:2 * QB]
        else:
            kw = kb[r0:r0 + 2 * QB, :]
            vw = vb[r0:r0 + 2 * QB, :]
            bias = bias_ref[p, 1 if u % nblk == 0 else 0]
        s = _dot_nt(qv, kw) + bias
        m = jnp.max(s, axis=1, keepdims=True)
        pv = _dot(jnp.exp(s - m).astype(BF16), vw)
        den = pv[:, A_DH:2 * A_DH]
        dst_o[r0:r0 + QB, :] = pv[:, 0:A_DH] / den
        dst_l[r0:r0 + QB, :] = m + jnp.log(den)

    load_operands(0, lambda a: srcs[a][...])
    for u in range(nunits):
        unit(u, 0, nunits, on.at[0], ln.at[0])

    for a in range(3):
        for r in range(STRIDE):
            p4[a, r * G:(r + 1) * G, :] = srcs[a][pl.ds(r, G, stride=STRIDE), :]
    load_operands(1, lambda a: p4[a])
    for u in range(nunits):
        unit(u, 1, G // QB, ops.at[0], lps.at[0])
    for r in range(STRIDE):
        on[1, pl.ds(r, G, stride=STRIDE), :] = ops[0, r * G:(r + 1) * G, :]
        ln[1, pl.ds(r, G, stride=STRIDE), :] = lps[0, r * G:(r + 1) * G, :]

    for a in range(3):
        for r in range(STRIDE):
            for c in range(STRIDE):
                u = r * STRIDE + c
                p16[a, u * QB:(u + 1) * QB, :] = p4[a, pl.ds(r * G + c, QB, stride=STRIDE), :]
    load_operands(2, lambda a: p16[a])
    for u in range(nunits):
        unit(u, 2, 1, ops.at[1], lps.at[1])
    for r in range(STRIDE):
        for c in range(STRIDE):
            u = r * STRIDE + c
            o4[pl.ds(r * G + c, QB, stride=STRIDE), :] = ops[1, u * QB:(u + 1) * QB, :]
            l4[pl.ds(r * G + c, QB, stride=STRIDE), :] = lps[1, u * QB:(u + 1) * QB, :]
    for r in range(STRIDE):
        on[2, pl.ds(r, G, stride=STRIDE), :] = o4[r * G:(r + 1) * G, :]
        ln[2, pl.ds(r, G, stride=STRIDE), :] = l4[r * G:(r + 1) * G, :]

    def combine(bi, carry):
        rows = pl.ds(pl.multiple_of(bi * QB, QB), QB)
        ls = [ln[p, rows, :] for p in range(len(PATTERNS))]
        lmax = functools.reduce(jnp.maximum, ls)
        tot = jnp.zeros((QB, A_DH), F32)
        wsum = jnp.zeros((QB, A_DH), F32)
        for p in range(len(PATTERNS)):
            a = jnp.exp(ls[p] - lmax)
            tot = tot + a * on[p, rows, :]
            wsum = wsum + a
        o_ref[rows, :] = (tot / wsum).astype(o_ref.dtype)
        return carry

    lax.fori_loop(0, nunits, combine, 0)


def _attn_prompt(proj, k_stack, v_stack, bias, l, B, S, qcol):
    assert S == N_STEPS * PATTERNS[-1][1] and PATTERNS[1][1] == STRIDE and PATTERNS[2][1] == STRIDE * STRIDE
    H = A_HEADS
    cb = qcol // A_DH
    kern = functools.partial(_attn_kernel, S=S)
    npat = len(PATTERNS)
    return pl.pallas_call(
        kern,
        grid=(B, H),
        in_specs=[
            pl.BlockSpec((S, A_DH), lambda b, h: (b, cb + h)),
            pl.BlockSpec((None, S, A_DH), lambda b, h: (l, b, h)),
            pl.BlockSpec((None, S, A_DH), lambda b, h: (l, b, h)),
            pl.BlockSpec((None, npat, 2, Q_BLOCK, 2 * Q_BLOCK), lambda b, h: (h, 0, 0, 0, 0)),
        ],
        out_specs=pl.BlockSpec((S, A_DH), lambda b, h: (b, h)),
        out_shape=jax.ShapeDtypeStruct((B * S, A_WIDTH), BF16),
        scratch_shapes=[
            pltpu.VMEM((3, S, A_DH), F32),
            pltpu.VMEM((3, S, A_DH), F32),
            pltpu.VMEM((npat, S, A_DH), BF16),
            pltpu.VMEM((npat, S + Q_BLOCK, A_DH), BF16),
            pltpu.VMEM((npat, S + Q_BLOCK, 2 * A_DH), BF16),
            pltpu.VMEM((2, S, A_DH), F32),
            pltpu.VMEM((2, S, A_DH), F32),
            pltpu.VMEM((S, A_DH), F32),
            pltpu.VMEM((S, A_DH), F32),
            pltpu.VMEM((npat, S, A_DH), F32),
            pltpu.VMEM((npat, S, A_DH), F32),
        ],
        compiler_params=_params(("parallel", "parallel")),
        name="attn_prompt",
    )(proj, k_stack, v_stack, bias)


def _attn_s_kernel(q_ref, kn_ref, vn_ref, kc_ref, vc_ref, bc_ref, bn_ref, o_ref, *, Lbuf):
    H = A_HEADS
    npat = len(PATTERNS)
    for h in range(H):
        hs = slice(h * A_DH, (h + 1) * A_DH)
        q = q_ref[:, hs] * (A_DH ** -0.5)
        kc = kc_ref[pl.ds(h, Lbuf, stride=H), :]
        vc = vc_ref[pl.ds(h, Lbuf, stride=H), :]
        s_c = _dot_nt(q, kc)
        s_n = _dot_nt(q, kn_ref[:, hs])
        mx = None
        for p in range(npat):
            mp = jnp.maximum(jnp.max(s_c + bc_ref[h, p], axis=1, keepdims=True),
                             jnp.max(s_n + bn_ref[h, p], axis=1, keepdims=True))
            mx = mp if mx is None else jnp.maximum(mx, mp)
        e_c = jnp.exp(s_c + bc_ref[h, 0] - mx)
        e_n = jnp.exp(s_n + bn_ref[h, 0] - mx)
        for p in range(1, npat):
            e_c = e_c + jnp.exp(s_c + bc_ref[h, p] - mx)
            e_n = e_n + jnp.exp(s_n + bn_ref[h, p] - mx)
        num = _dot(e_c, vc) + _dot(e_n, vn_ref[:, hs])
        den = jnp.sum(e_c, axis=1, keepdims=True) + jnp.sum(e_n, axis=1, keepdims=True)
        o_ref[:, hs] = num / den


def _attn_sample(proj, k_stack, v_stack, cache_k, cache_v, bias_c, bias_n, l, B, T, qcol):
    H = A_HEADS
    Lbuf = cache_k.shape[2] // H
    npat = len(PATTERNS)
    kern = functools.partial(_attn_s_kernel, Lbuf=Lbuf)
    return pl.pallas_call(
        kern,
        grid=(B,),
        in_specs=[
            pl.BlockSpec((T, A_WIDTH), lambda b: (b, qcol // A_WIDTH)),
            pl.BlockSpec((None, T, A_WIDTH), lambda b: (l, b, 0)),
            pl.BlockSpec((None, T, A_WIDTH), lambda b: (l, b, 0)),
            pl.BlockSpec((None, None, Lbuf * H, A_DH), lambda b: (l, b, 0, 0)),
            pl.BlockSpec((None, None, Lbuf * H, A_DH), lambda b: (l, b, 0, 0)),
            pl.BlockSpec((H, npat, T, Lbuf), lambda b: (0, 0, 0, 0)),
            pl.BlockSpec((H, npat, T, T), lambda b: (0, 0, 0, 0)),
        ],
        out_specs=pl.BlockSpec((T, A_WIDTH), lambda b: (b, 0)),
        out_shape=jax.ShapeDtypeStruct((B * T, A_WIDTH), F32),
        compiler_params=_params(("parallel",)),
        name="attn_sample",
    )(proj, k_stack, v_stack, cache_k, cache_v, bias_c, bias_n)


def _outproj_kernel(x_ref, hm_ref, ha_ref, gm_ref, ga_ref, wpm_ref, wpa_ref, wo_ref, o_ref):
    pm = _dot(hm_ref[...].astype(BF16), wpm_ref[...])
    pa = _dot(ha_ref[...].astype(BF16), wpa_ref[...])
    merged = jax.nn.sigmoid(gm_ref[...]) * pm + jax.nn.sigmoid(ga_ref[...]) * pa
    o_ref[...] = x_ref[...] + _dot(merged.astype(BF16), wo_ref[...])


def _outproj(x, hm, ha, proj, w_pm, w_pa, w_out, l, tm):
    M, D = x.shape
    once = pl.Buffered(1)
    return pl.pallas_call(
        _outproj_kernel,
        grid=(M // tm,),
        in_specs=[
            pl.BlockSpec((tm, D), lambda i: (i, 0)),
            pl.BlockSpec((tm, M_WIDTH), lambda i: (i, 0)),
            pl.BlockSpec((tm, A_WIDTH), lambda i: (i, 0)),
            pl.BlockSpec((tm, D), lambda i: (i, 0)),
            pl.BlockSpec((tm, D), lambda i: (i, 1)),
            pl.BlockSpec((None, M_WIDTH, D), lambda i: (l, 0, 0), pipeline_mode=once),
            pl.BlockSpec((None, A_WIDTH, D), lambda i: (l, 0, 0), pipeline_mode=once),
            pl.BlockSpec((None, D, D), lambda i: (l, 0, 0), pipeline_mode=once),
        ],
        out_specs=pl.BlockSpec((tm, D), lambda i: (i, 0)),
        out_shape=jax.ShapeDtypeStruct((M, D), F32),
        compiler_params=_params(("parallel",)),
        name="outproj",
    )(x, hm, ha, proj, proj, w_pm, w_pa, w_out)


def _norm_kernel(x_ref, g_ref, o_ref):
    o_ref[...] = _rms(x_ref[...], g_ref[...])


def _final_norm(x, g, tm):
    M, D = x.shape
    return pl.pallas_call(
        _norm_kernel,
        grid=(M // tm,),
        in_specs=[pl.BlockSpec((tm, D), lambda i: (i, 0)), pl.BlockSpec((1, D), lambda i: (0, 0))],
        out_specs=pl.BlockSpec((tm, D), lambda i: (i, 0)),
        out_shape=jax.ShapeDtypeStruct((M, D), F32),
        compiler_params=_params(("parallel",)),
        name="final_norm",
    )(x, g)


def _t5_bucket(dist):
    exact = N_BUCKETS // 2
    d32 = jnp.maximum(dist, 1).astype(F32)
    large = exact + (jnp.log(d32 / exact) / math.log(MAX_DISTANCE / exact) * (N_BUCKETS - exact)).astype(jnp.int32)
    large = jnp.minimum(large, N_BUCKETS - 1)
    return jnp.where(dist < exact, dist, large)


def _step_biases(rel_table):
    assert all(w // d == N_STEPS for w, d in PATTERNS)
    buckets = jnp.stack([_t5_bucket(d * jnp.arange(N_STEPS + 1, dtype=jnp.int32)) for (w, d) in PATTERNS])
    onehot = (buckets[:, :, None] == jnp.arange(N_BUCKETS)[None, None, :]).astype(F32)
    return jnp.einsum("pjn,nh->phj", onehot, rel_table.astype(F32), precision=lax.Precision.HIGHEST)


def _toeplitz(g, rows, cols, off):
    n = g.shape[-1]
    width = off + cols
    assert width <= n - 1 and off - (rows - 1) >= 0
    flat = jnp.tile(g, (1,) * (g.ndim - 1) + (rows,))[..., :rows * (n - 1)]
    return flat.reshape(g.shape[:-1] + (rows, n - 1))[..., off:off + cols]


def _prompt_bias(sb):
    Q = Q_BLOCK
    ninf = lambda n: jnp.full(sb.shape[:-1] + (n,), -jnp.inf, F32)
    g = jnp.concatenate([ninf(Q - 1), sb[..., ::-1], ninf(Q)], axis=-1)
    base = _toeplitz(g, Q, 2 * Q, Q - 1)
    first = jnp.where(jnp.arange(2 * Q) >= Q, base, -jnp.inf)
    return jnp.transpose(jnp.stack([base, first], axis=2), (1, 0, 2, 3, 4))


def _sample_bias(sb, T, Lbuf):
    n = Lbuf + T
    tabs = []
    for p, (w, d) in enumerate(PATTERNS):
        bd = jnp.concatenate([sb[p][..., None], jnp.full(sb[p].shape + (d - 1,), -jnp.inf, F32)], axis=-1)
        bd = bd.reshape(sb.shape[1], -1)
        bd = jnp.concatenate([bd, jnp.full((sb.shape[1], n), -jnp.inf, F32)], axis=-1)[:, :n]
        g = jnp.concatenate([bd[:, ::-1], jnp.full((sb.shape[1], T), -jnp.inf, F32)], axis=-1)
        tabs.append(_toeplitz(g, T, n, T - 1))
    tab = jnp.stack(tabs, axis=1)
    return tab[..., :Lbuf], tab[..., Lbuf:]


def _row_tile(M, cap):
    return cap if M % cap == 0 else M


def _trunk(x3, wts, sb, states):
    B, T, D = x3.shape
    M = B * T
    depth = wts["w_ab"].shape[0]
    prompt = states is None
    FF = wts["w_ffa_out"].shape[1]
    tm = _row_tile(M, 1024)
    tf = 512 if FF % 512 == 0 else FF
    tn = W_TILE
    tm_out = _row_tile(M, 512)
    col_m = 2 * D
    col_q = 2 * D + QK_WIDTH + 2 * M_WIDTH
    if prompt:
        bias = _prompt_bias(sb)
        cbuf = jnp.zeros((1, B, CONV_PAD, QK_WIDTH), F32)
        C0 = jnp.zeros((1, B, M_HEADS, M_DK, M_DV), F32)
        n0 = jnp.zeros((1, B, M_HEADS, 1, M_DK), F32)
        m0 = jnp.zeros((1, B, M_HEADS, 1, 1), F32)
    else:
        cache_k, cache_v, sC, sn, sm, sconv = states
        Lbuf = cache_k.shape[2]
        bias_c, bias_n = _sample_bias(sb, T, Lbuf)
        cache_k = cache_k.astype(F32).reshape(depth, B, Lbuf * A_HEADS, A_DH)
        cache_v = cache_v.astype(F32).reshape(depth, B, Lbuf * A_HEADS, A_DH)
        cbuf = jnp.pad(sconv.astype(F32), ((0, 0), (0, 0), (CONV_PAD - (CONV_W - 1), 0), (0, 0)))
        C0 = sC.astype(F32)
        n0 = sn.astype(F32).reshape(depth, B, M_HEADS, 1, M_DK)
        m0 = sm.astype(F32).reshape(depth, B, M_HEADS, 1, 1)

    x = x3.reshape(M, D)
    k_stack = jnp.zeros((depth, M, A_WIDTH), F32)
    v_stack = jnp.zeros((depth, M, A_WIDTH), F32)
    Cs, ns, ms, convs = [], [], [], []
    for l in range(depth):
        x = _ffn(x, wts["ln_ffa"], wts["w_ffa_in"], wts["w_ffa_out"], l, tm, tf)
        proj, gif = _inproj(x, wts["ln_mix"], wts["w_ab"], wts["w_if"], l, tm, tn)
        k_stack, v_stack = _kvproj(x, wts["ln_mix"], wts["w_ab"], k_stack, v_stack, l, tm, tn)
        ls = 0 if prompt else l
        hm, C1, n1, m1 = _mlstm(proj, gif, wts["b_if"], wts["conv_w"], wts["conv_b"], wts["m_norm"],
                                cbuf, C0, n0, m0, l, ls, B, T, col_m,
                                BF16 if prompt else F32, BF16 if prompt else F32)
        if prompt:
            ha = _attn_prompt(proj, k_stack, v_stack, bias, l, B, T, col_q)
        else:
            ha = _attn_sample(proj, k_stack, v_stack, cache_k, cache_v, bias_c, bias_n, l, B, T, col_q)
        x = _outproj(x, hm, ha, proj, wts["w_pm"], wts["w_pa"], wts["w_out"], l, tm_out)
        x = _ffn(x, wts["ln_ffb"], wts["w_ffb_in"], wts["w_ffb_out"], l, tm, tf)
        Cs.append(C1)
        ns.append(n1.reshape(B, M_HEADS, M_DK))
        ms.append(m1.reshape(B, M_HEADS))
        pre = proj.reshape(B, T, -1)[:, :, col_m:col_m + QK_WIDTH]
        if T >= CONV_W - 1:
            convs.append(pre[:, T - (CONV_W - 1):])
        else:
            convs.append(jnp.concatenate([cbuf[ls][:, CONV_PAD - (CONV_W - 1) + T:], pre], axis=1))
    y = _final_norm(x, wts["ln_f"], tm).reshape(B, T, D)
    stk = lambda xs: jnp.stack(xs, axis=0)
    keep = min(MAX_WINDOW, T)
    k_new = k_stack.reshape(depth, B, T, A_HEADS, A_DH)[:, :, T - keep:]
    v_new = v_stack.reshape(depth, B, T, A_HEADS, A_DH)[:, :, T - keep:]
    return y, (k_new, v_new, stk(Cs), stk(ns), stk(ms), stk(convs))


def kernel(x_prompt, x_sample, cache_k, cache_v, state_C, state_n, state_m, state_conv, w_in, conv_w, conv_b, b_if, m_norm, w_pm, w_pa, w_out, rel_table, ln_ffa, w_ffa_in, w_ffa_out, ln_mix, ln_ffb, w_ffb_in, w_ffb_out, ln_f):
    depth, D, _ = w_in.shape
    o_if = QK_WIDTH + 2 * M_WIDTH
    o_a = o_if + 2 * M_HEADS
    w_if32 = jnp.pad(w_in[:, :, o_if:o_a], ((0, 0), (0, 0), (0, LANE - 2 * M_HEADS)))
    w_if_hi = w_if32.astype(BF16)
    w_if_lo = (w_if32 - w_if_hi.astype(F32)).astype(BF16)
    wts = dict(
        w_ab=_wprep(w_in, o_if, o_a - o_if, W_TILE),
        w_if=jnp.stack([w_if_hi, w_if_lo], axis=1),
        b_if=jnp.pad(b_if.astype(F32), ((0, 0), (0, LANE - 2 * M_HEADS))).reshape(depth, 1, LANE),
        conv_w=conv_w.astype(F32),
        conv_b=conv_b.astype(F32).reshape(depth, 1, QK_WIDTH),
        m_norm=m_norm.astype(F32).reshape(depth, 1, M_WIDTH),
        w_pm=w_pm.astype(BF16), w_pa=w_pa.astype(BF16), w_out=w_out.astype(BF16),
        ln_ffa=ln_ffa.astype(F32).reshape(depth, 1, D), ln_mix=ln_mix.astype(F32).reshape(depth, 1, D),
        ln_ffb=ln_ffb.astype(F32).reshape(depth, 1, D), ln_f=ln_f.astype(F32).reshape(1, D),
        w_ffa_in=w_ffa_in.astype(BF16), w_ffa_out=w_ffa_out.astype(BF16),
        w_ffb_in=w_ffb_in.astype(BF16), w_ffb_out=w_ffb_out.astype(BF16),
    )
    sb = _step_biases(rel_table)
    y_p, (k_p, v_p, C_p, n_p, m_p, conv_p) = _trunk(x_prompt, wts, sb, None)
    y_s, (k_s, v_s, C_s, n_s, m_s, conv_s) = _trunk(
        x_sample, wts, sb, (cache_k, cache_v, state_C, state_n, state_m, state_conv))
    return (y_p, y_s, k_p, v_p, C_p, n_p, m_p, conv_p, k_s, v_s, C_s, n_s, m_s, conv_s)
```

```python
import functools
import math

import jax
import jax.numpy as jnp
from jax import lax
from jax.experimental import pallas as pl
from jax.experimental.pallas import tpu as pltpu

F32 = jnp.float32
BF16 = jnp.bfloat16

M_HEADS = 4
M_DK = 256
M_DV = 256
M_WIDTH = M_HEADS * M_DV
QK_WIDTH = 2 * M_HEADS * M_DK
CONV_W = 4
CHUNK = 128
A_HEADS = 8
A_DH = 128
A_WIDTH = A_HEADS * A_DH
PATTERNS = ((128, 1), (512, 4), (2048, 16))
MAX_WINDOW = 2048
Q_BLOCK = 128
N_BUCKETS = 32
MAX_DISTANCE = MAX_WINDOW
EPS = 1e-6
N_STEPS = 128
LANE = 128
CONV_PAD = 8
VMEM_LIMIT = 58 * 1024 * 1024


def _dot(a, b):
    return jnp.dot(a, b, preferred_element_type=F32)


def _dot_nt(a, b):
    return lax.dot_general(a, b, (((1,), (1,)), ((), ())), preferred_element_type=F32)


def _dot_tn(a, b):
    return lax.dot_general(a, b, (((0,), (0,)), ((), ())), preferred_element_type=F32)


def _rms(x, g):
    return x * lax.rsqrt(jnp.mean(x * x, axis=-1, keepdims=True) + EPS) * g


def _params(sem):
    return pltpu.CompilerParams(dimension_semantics=sem, vmem_limit_bytes=VMEM_LIMIT)


def _ffn_kernel(x_ref, g_ref, wg_ref, wu_ref, wo_ref, o_ref, h_ref):
    j = pl.program_id(1)

    @pl.when(j == 0)
    def _():
        h_ref[...] = _rms(x_ref[...], g_ref[...]).astype(BF16)
        o_ref[...] = jnp.zeros_like(o_ref)

    h = h_ref[...]
    gate = _dot(h, wg_ref[...])
    up = _dot(h, wu_ref[...])
    act = (gate * jax.nn.sigmoid(gate) * up).astype(BF16)
    o_ref[...] += _dot(act, wo_ref[...])

    @pl.when(j == pl.num_programs(1) - 1)
    def _():
        o_ref[...] = x_ref[...] + 0.5 * o_ref[...]


def _ffn(x, g, w_in, w_out, l, tm, tf):
    M, D = x.shape
    FF = w_out.shape[1]
    nj = FF // tf
    return pl.pallas_call(
        _ffn_kernel,
        grid=(M // tm, nj),
        in_specs=[
            pl.BlockSpec((tm, D), lambda i, j: (i, 0)),
            pl.BlockSpec((None, 1, D), lambda i, j: (l, 0, 0)),
            pl.BlockSpec((None, D, tf), lambda i, j: (l, 0, j)),
            pl.BlockSpec((None, D, tf), lambda i, j: (l, 0, j + nj)),
            pl.BlockSpec((None, tf, D), lambda i, j: (l, j, 0)),
        ],
        out_specs=pl.BlockSpec((tm, D), lambda i, j: (i, 0)),
        out_shape=jax.ShapeDtypeStruct((M, D), F32),
        scratch_shapes=[pltpu.VMEM((tm, D), BF16)],
        compiler_params=_params(("parallel", "arbitrary")),
        name="ffn",
    )(x, g, w_in, w_in, w_out)


W_TILE = 1024
SUBLANE = 8


def _wprep_kernel(w1_ref, w2_ref, o_ref, *, nA, shift):
    j = pl.program_id(1)

    @pl.when(j < nA)
    def _():
        o_ref[...] = w1_ref[...].astype(BF16)

    @pl.when(j >= nA)
    def _():
        o_ref[...] = jnp.concatenate([w1_ref[shift:, :], w2_ref[...]], axis=0).astype(BF16)


def _wprep(w_t, n_a, shift, tn):
    depth, n_in, D = w_t.shape
    assert n_a % tn == 0 and (n_in - shift) % tn == 0 and shift == SUBLANE
    nj = (n_in - shift) // tn
    kern = functools.partial(_wprep_kernel, nA=n_a // tn, shift=shift)
    return pl.pallas_call(
        kern,
        grid=(depth, nj),
        in_specs=[
            pl.BlockSpec((None, tn, D), lambda l, j: (l, j, 0)),
            pl.BlockSpec((None, shift, D), lambda l, j: (l, (j + 1) * (tn // shift), 0)),
        ],
        out_specs=pl.BlockSpec((None, tn, D), lambda l, j: (l, j, 0)),
        out_shape=jax.ShapeDtypeStruct((depth, n_in - shift, D), BF16),
        compiler_params=_params(("parallel", "parallel")),
        name="wprep",
    )(w_t, w_t)


def _inproj_kernel(x_ref, g_ref, w_ref, wif_ref, o_ref, gif_ref, h_ref):
    j = pl.program_id(1)

    @pl.when(j == 0)
    def _():
        h32 = _rms(x_ref[...], g_ref[...])
        hb = h32.astype(BF16)
        h_ref[...] = hb
        hl = (h32 - hb.astype(F32)).astype(BF16)
        gif_ref[...] = _dot_nt(hb, wif_ref[0]) + _dot_nt(hb, wif_ref[1]) + _dot_nt(hl, wif_ref[0])

    o_ref[...] = _dot_nt(h_ref[...], w_ref[...])


def _inproj(x, g, w_ab, w_if, l, tm, tn):
    M, D = x.shape
    nG, nA, nQ = 2 * D // tn, (QK_WIDTH + 2 * M_WIDTH) // tn, A_WIDTH // tn
    NP = 2 * D + QK_WIDTH + 2 * M_WIDTH + A_WIDTH
    w_idx = lambda j: jnp.where(j < nG, j + nA + 3 * nQ, j - nG)
    return pl.pallas_call(
        _inproj_kernel,
        grid=(M // tm, NP // tn),
        in_specs=[
            pl.BlockSpec((tm, D), lambda i, j: (i, 0)),
            pl.BlockSpec((None, 1, D), lambda i, j: (l, 0, 0)),
            pl.BlockSpec((None, tn, D), lambda i, j: (l, w_idx(j), 0)),
            pl.BlockSpec((None, 2, LANE, D), lambda i, j: (l, 0, 0, 0)),
        ],
        out_specs=[
            pl.BlockSpec((tm, tn), lambda i, j: (i, j)),
            pl.BlockSpec((tm, LANE), lambda i, j: (i, 0)),
        ],
        out_shape=[jax.ShapeDtypeStruct((M, NP), F32), jax.ShapeDtypeStruct((M, LANE), F32)],
        scratch_shapes=[pltpu.VMEM((tm, D), BF16)],
        compiler_params=_params(("parallel", "arbitrary")),
        name="inproj",
    )(x, g, w_ab, w_if)


def _kvproj_kernel(x_ref, g_ref, w_ref, kin_ref, vin_ref, k_ref, v_ref, h_ref, *, nQ):
    del kin_ref, vin_ref
    j = pl.program_id(1)

    @pl.when(j == 0)
    def _():
        h_ref[...] = _rms(x_ref[...], g_ref[...]).astype(BF16)

    @pl.when(j < nQ)
    def _():
        k_ref[...] = _dot_nt(h_ref[...], w_ref[...])

    @pl.when(j >= nQ)
    def _():
        v_ref[...] = _dot_nt(h_ref[...], w_ref[...])


def _kvproj(x, g, w_ab, k_stack, v_stack, l, tm, tn):
    M, D = x.shape
    nA, nQ = (QK_WIDTH + 2 * M_WIDTH) // tn, A_WIDTH // tn
    kern = functools.partial(_kvproj_kernel, nQ=nQ)
    return pl.pallas_call(
        kern,
        grid=(M // tm, 2 * nQ),
        in_specs=[
            pl.BlockSpec((tm, D), lambda i, j: (i, 0)),
            pl.BlockSpec((None, 1, D), lambda i, j: (l, 0, 0)),
            pl.BlockSpec((None, tn, D), lambda i, j: (l, nA + nQ + j, 0)),
            pl.BlockSpec(memory_space=pl.ANY),
            pl.BlockSpec(memory_space=pl.ANY),
        ],
        out_specs=[
            pl.BlockSpec((None, tm, tn), lambda i, j: (l, i, jnp.minimum(j, nQ - 1))),
            pl.BlockSpec((None, tm, tn), lambda i, j: (l, i, jnp.maximum(j - nQ, 0))),
        ],
        out_shape=[jax.ShapeDtypeStruct(k_stack.shape, F32), jax.ShapeDtypeStruct(v_stack.shape, F32)],
        scratch_shapes=[pltpu.VMEM((tm, D), BF16)],
        input_output_aliases={3: 0, 4: 1},
        compiler_params=_params(("parallel", "arbitrary")),
        name="kvproj",
    )(x, g, w_ab, k_stack, v_stack)


def _mlstm_kernel(qp_ref, kp_ref, v_ref, og_ref, gif_ref, bif_ref, cw_ref, cb_ref, mn_ref,
                  cbuf_ref, C0_ref, n0_ref, m0_ref,
                  hm_ref, C_ref, n_ref, m_ref, xq_ref, xk_ref, *, L, mm_dtype):
    c = pl.program_id(1)
    half = QK_WIDTH // 2

    @pl.when(c == 0)
    def _():
        xq_ref[0:CONV_PAD, :] = cbuf_ref[:, 0:half]
        xk_ref[0:CONV_PAD, :] = cbuf_ref[:, half:QK_WIDTH]
        C_ref[...] = C0_ref[...]
        n_ref[...] = n0_ref[...]
        m_ref[...] = m0_ref[...]

    @pl.when(c > 0)
    def _():
        xq_ref[0:CONV_PAD, :] = xq_ref[L:L + CONV_PAD, :]
        xk_ref[0:CONV_PAD, :] = xk_ref[L:L + CONV_PAD, :]

    xq_ref[CONV_PAD:CONV_PAD + L, :] = qp_ref[...]
    xk_ref[CONV_PAD:CONV_PAD + L, :] = kp_ref[...]

    cw = cw_ref[...]
    cb = cb_ref[...]

    def conv_silu(x_ref, w, b):
        xs = x_ref[...]
        y = b + xs * w[CONV_W - 1:CONV_W, :]
        for s in range(1, CONV_W):
            y = y + pltpu.roll(xs, s, axis=0) * w[CONV_W - 1 - s:CONV_W - s, :]
        y = y[CONV_PAD:, :]
        return y * jax.nn.sigmoid(y)

    qa = conv_silu(xq_ref, cw[:, 0:half], cb[:, 0:half])
    ka = conv_silu(xk_ref, cw[:, half:QK_WIDTH], cb[:, half:QK_WIDTH]) * (M_DK ** -0.5)
    g = gif_ref[...] + bif_ref[...]

    ti = lax.broadcasted_iota(jnp.int32, (L, L), 0)
    si = lax.broadcasted_iota(jnp.int32, (L, L), 1)
    eye = ti == si
    low = si <= ti
    upp = ti <= si

    for h in range(M_HEADS):
        sl = slice(h * M_DV, (h + 1) * M_DV)
        ig_col = g[:, h:h + 1]
        gf = g[:, M_HEADS + h:M_HEADS + h + 1]
        lf_col = jnp.minimum(gf, 0.0) - jnp.log1p(jnp.exp(-jnp.abs(gf)))
        ig_row = jnp.sum(jnp.where(eye, ig_col, 0.0), axis=0, keepdims=True)
        lf_row = jnp.sum(jnp.where(eye, lf_col, 0.0), axis=0, keepdims=True)
        F_col = jnp.sum(jnp.where(low, lf_row, 0.0), axis=1, keepdims=True)
        F_row = jnp.sum(jnp.where(upp, lf_col, 0.0), axis=0, keepdims=True)
        m0 = m_ref[h]
        C0 = C_ref[h]
        n0 = n_ref[h]
        logD = jnp.where(low, F_col - F_row + ig_row, -jnp.inf)
        lst = F_col + m0
        m_col = jnp.maximum(lst, jnp.max(logD, axis=1, keepdims=True))
        q = qa[:, sl]
        k = ka[:, sl]
        qb = q.astype(mm_dtype)
        kb = k.astype(mm_dtype)
        vb = v_ref[:, sl].astype(mm_dtype)
        wgt = jnp.exp(logD - m_col) * _dot_nt(qb, kb)
        sc = jnp.exp(lst - m_col)
        num = _dot(wgt.astype(mm_dtype), vb) + sc * _dot(qb, C0.astype(mm_dtype))
        den = jnp.sum(wgt, axis=1, keepdims=True) + sc * jnp.sum(q * n0, axis=1, keepdims=True)
        hh = num / jnp.maximum(jnp.abs(den), jnp.exp(-m_col))
        FT = F_col[L - 1:L, :]
        lT = FT + m0
        m_new = jnp.maximum(lT, jnp.max(FT - F_row + ig_row, axis=1, keepdims=True))
        e_col = jnp.exp(FT - F_col + ig_col - m_new)
        sT = jnp.exp(lT - m_new)
        ke = k * e_col
        C_ref[h] = sT * C0 + _dot_tn(ke.astype(mm_dtype), vb)
        n_ref[h] = sT * n0 + jnp.sum(ke, axis=0, keepdims=True)
        m_ref[h] = m_new
        hn = hh * lax.rsqrt(jnp.mean(hh * hh, axis=1, keepdims=True) + EPS)
        out = jax.nn.sigmoid(og_ref[:, sl]) * hn * mn_ref[:, sl]
        hm_ref[:, sl] = out.astype(hm_ref.dtype)


def _mlstm(proj, gif, b_if, conv_w, conv_b, m_norm, cbuf, C0, n0, m0, l, ls, B, T, col0, out_dtype, mm_dtype):
    L = CHUNK if T % CHUNK == 0 else T
    nc = T // L
    qo = col0 // M_WIDTH
    kern = functools.partial(_mlstm_kernel, L=L, mm_dtype=mm_dtype)
    row = lambda b, c: b * nc + c
    return pl.pallas_call(
        kern,
        grid=(B, nc),
        in_specs=[
            pl.BlockSpec((L, M_WIDTH), lambda b, c: (row(b, c), qo)),
            pl.BlockSpec((L, M_WIDTH), lambda b, c: (row(b, c), qo + 1)),
            pl.BlockSpec((L, M_WIDTH), lambda b, c: (row(b, c), qo + 2)),
            pl.BlockSpec((L, M_WIDTH), lambda b, c: (row(b, c), qo + 3)),
            pl.BlockSpec((L, LANE), lambda b, c: (row(b, c), 0)),
            pl.BlockSpec((None, 1, LANE), lambda b, c: (l, 0, 0)),
            pl.BlockSpec((None, CONV_W, QK_WIDTH), lambda b, c: (l, 0, 0)),
            pl.BlockSpec((None, 1, QK_WIDTH), lambda b, c: (l, 0, 0)),
            pl.BlockSpec((None, 1, M_WIDTH), lambda b, c: (l, 0, 0)),
            pl.BlockSpec((None, None, CONV_PAD, QK_WIDTH), lambda b, c: (ls, b, 0, 0)),
            pl.BlockSpec((None, None, M_HEADS, M_DK, M_DV), lambda b, c: (ls, b, 0, 0, 0)),
            pl.BlockSpec((None, None, M_HEADS, 1, M_DK), lambda b, c: (ls, b, 0, 0, 0)),
            pl.BlockSpec((None, None, M_HEADS, 1, 1), lambda b, c: (ls, b, 0, 0, 0)),
        ],
        out_specs=[
            pl.BlockSpec((L, M_WIDTH), lambda b, c: (row(b, c), 0)),
            pl.BlockSpec((None, M_HEADS, M_DK, M_DV), lambda b, c: (b, 0, 0, 0)),
            pl.BlockSpec((None, M_HEADS, 1, M_DK), lambda b, c: (b, 0, 0, 0)),
            pl.BlockSpec((None, M_HEADS, 1, 1), lambda b, c: (b, 0, 0, 0)),
        ],
        out_shape=[
            jax.ShapeDtypeStruct((B * T, M_WIDTH), out_dtype),
            jax.ShapeDtypeStruct((B, M_HEADS, M_DK, M_DV), F32),
            jax.ShapeDtypeStruct((B, M_HEADS, 1, M_DK), F32),
            jax.ShapeDtypeStruct((B, M_HEADS, 1, 1), F32),
        ],
        scratch_shapes=[pltpu.VMEM((L + CONV_PAD, M_WIDTH), F32), pltpu.VMEM((L + CONV_PAD, M_WIDTH), F32)],
        compiler_params=_params(("parallel", "arbitrary")),
        name="mlstm",
    )(proj, proj, proj, proj, gif, b_if, conv_w, conv_b, m_norm, cbuf, C0, n0, m0)


STRIDE = 4


def _attn_kernel(q_ref, k_ref, v_ref, bias_ref, o_ref,
                 p4, p16, qbs, kbs, vbs, ops, lps, o4, l4, on, ln, *, S):
    scale = A_DH ** -0.5
    QB = Q_BLOCK
    nunits = S // QB
    G = S // STRIDE
    srcs = (q_ref, k_ref, v_ref)

    def load_operands(p, get):
        qb, kb, vb = qbs.at[p], kbs.at[p], vbs.at[p]
        kb[0:QB, :] = jnp.zeros((QB, A_DH), BF16)
        vb[0:QB, 0:A_DH] = jnp.zeros((QB, A_DH), BF16)
        vb[:, A_DH:2 * A_DH] = jnp.ones((S + QB, A_DH), BF16)
        qb[...] = (get(0) * scale).astype(BF16)
        kb[QB:QB + S, :] = get(1).astype(BF16)
        vb[QB:QB + S, 0:A_DH] = get(2).astype(BF16)

    def unit(u, p, nblk, dst_o, dst_l):
        qb, kb, vb = qbs.at[p], kbs.at[p], vbs.at[p]
        r0 = u * QB
        qv = qb[r0:r0 + QB, :]
        if nblk == 1:
            kw = kb[QB + r0:2 * QB + r0, :]
            vw = vb[QB + r0:2 * QB + r0, :]
            bias = bias_ref[p, 0, :, QB:2 * QB]
        else:
            kw = kb[r0:r0 + 2 * QB, :]
            vw = vb[r0:r0 + 2 * QB, :]
            bias = bias_ref[p, 1 if u % nblk == 0 else 0]
        s = _dot_nt(qv, kw) + bias
        m = jnp.max(s, axis=1, keepdims=True)
        pv = _dot(jnp.exp(s - m).astype(BF16), vw)
        den = pv[:, A_DH:2 * A_DH]
        dst_o[r0:r0 + QB, :] = pv[:, 0:A_DH] / den
        dst_l[r0:r0 + QB, :] = m + jnp.log(den)

    load_operands(0, lambda a: srcs[a][...])
    for u in range(nunits):
        unit(u, 0, nunits, on.at[0], ln.at[0])

    for a in range(3):
        for r in range(STRIDE):
            p4[a, r * G:(r + 1) * G, :] = srcs[a][pl.ds(r, G, stride=STRIDE), :]
    load_operands(1, lambda a: p4[a])
    for u in range(nunits):
        unit(u, 1, G // QB, ops.at[0], lps.at[0])
    for r in range(STRIDE):
        on[1, pl.ds(r, G, stride=STRIDE), :] = ops[0, r * G:(r + 1) * G, :]
        ln[1, pl.ds(r, G, stride=STRIDE), :] = lps[0, r * G:(r + 1) * G, :]

    for a in range(3):
        for r in range(STRIDE):
            for c in range(STRIDE):
                u = r * STRIDE + c
                p16[a, u * QB:(u + 1) * QB, :] = p4[a, pl.ds(r * G + c, QB, stride=STRIDE), :]
    load_operands(2, lambda a: p16[a])
    for u in range(nunits):
        unit(u, 2, 1, ops.at[1], lps.at[1])
    for r in range(STRIDE):
        for c in range(STRIDE):
            u = r * STRIDE + c
            o4[pl.ds(r * G + c, QB, stride=STRIDE), :] = ops[1, u * QB:(u + 1) * QB, :]
            l4[pl.ds(r * G + c, QB, stride=STRIDE), :] = lps[1, u * QB:(u + 1) * QB, :]
    for r in range(STRIDE):
        on[2, pl.ds(r, G, stride=STRIDE), :] = o4[r * G:(r + 1) * G, :]
        ln[2, pl.ds(r, G, stride=STRIDE), :] = l4[r * G:(r + 1) * G, :]

    def combine(bi, carry):
        rows = pl.ds(pl.multiple_of(bi * QB, QB), QB)
        ls = [ln[p, rows, :] for p in range(len(PATTERNS))]
        lmax = functools.reduce(jnp.maximum, ls)
        tot = jnp.zeros((QB, A_DH), F32)
        wsum = jnp.zeros((QB, A_DH), F32)
        for p in range(len(PATTERNS)):
            a = jnp.exp(ls[p] - lmax)
            tot = tot + a * on[p, rows, :]
            wsum = wsum + a
        o_ref[rows, :] = (tot / wsum).astype(o_ref.dtype)
        return carry

    lax.fori_loop(0, nunits, combine, 0)


def _attn_prompt(proj, k_stack, v_stack, bias, l, B, S, qcol):
    assert S == N_STEPS * PATTERNS[-1][1] and PATTERNS[1][1] == STRIDE and PATTERNS[2][1] == STRIDE * STRIDE
    H = A_HEADS
    cb = qcol // A_DH
    kern = functools.partial(_attn_kernel, S=S)
    npat = len(PATTERNS)
    return pl.pallas_call(
        kern,
        grid=(B, H),
        in_specs=[
            pl.BlockSpec((S, A_DH), lambda b, h: (b, cb + h)),
            pl.BlockSpec((None, S, A_DH), lambda b, h: (l, b, h)),
            pl.BlockSpec((None, S, A_DH), lambda b, h: (l, b, h)),
            pl.BlockSpec((None, npat, 2, Q_BLOCK, 2 * Q_BLOCK), lambda b, h: (h, 0, 0, 0, 0)),
        ],
        out_specs=pl.BlockSpec((S, A_DH), lambda b, h: (b, h)),
        out_shape=jax.ShapeDtypeStruct((B * S, A_WIDTH), BF16),
        scratch_shapes=[
            pltpu.VMEM((3, S, A_DH), F32),
            pltpu.VMEM((3, S, A_DH), F32),
            pltpu.VMEM((npat, S, A_DH), BF16),
            pltpu.VMEM((npat, S + Q_BLOCK, A_DH), BF16),
            pltpu.VMEM((npat, S + Q_BLOCK, 2 * A_DH), BF16),
            pltpu.VMEM((2, S, A_DH), F32),
            pltpu.VMEM((2, S, A_DH), F32),
            pltpu.VMEM((S, A_DH), F32),
            pltpu.VMEM((S, A_DH), F32),
            pltpu.VMEM((npat, S, A_DH), F32),
            pltpu.VMEM((npat, S, A_DH), F32),
        ],
        compiler_params=_params(("parallel", "parallel")),
        name="attn_prompt",
    )(proj, k_stack, v_stack, bias)


def _attn_s_kernel(q_ref, kn_ref, vn_ref, kc_ref, vc_ref, bc_ref, bn_ref, o_ref, *, Lbuf):
    H = A_HEADS
    npat = len(PATTERNS)
    for h in range(H):
        hs = slice(h * A_DH, (h + 1) * A_DH)
        q = q_ref[:, hs] * (A_DH ** -0.5)
        kc = kc_ref[pl.ds(h, Lbuf, stride=H), :]
        vc = vc_ref[pl.ds(h, Lbuf, stride=H), :]
        s_c = _dot_nt(q, kc)
        s_n = _dot_nt(q, kn_ref[:, hs])
        mx = None
        for p in range(npat):
            mp = jnp.maximum(jnp.max(s_c + bc_ref[h, p], axis=1, keepdims=True),
                             jnp.max(s_n + bn_ref[h, p], axis=1, keepdims=True))
            mx = mp if mx is None else jnp.maximum(mx, mp)
        e_c = jnp.exp(s_c + bc_ref[h, 0] - mx)
        e_n = jnp.exp(s_n + bn_ref[h, 0] - mx)
        for p in range(1, npat):
            e_c = e_c + jnp.exp(s_c + bc_ref[h, p] - mx)
            e_n = e_n + jnp.exp(s_n + bn_ref[h, p] - mx)
        num = _dot(e_c, vc) + _dot(e_n, vn_ref[:, hs])
        den = jnp.sum(e_c, axis=1, keepdims=True) + jnp.sum(e_n, axis=1, keepdims=True)
        o_ref[:, hs] = num / den


def _attn_sample(proj, k_stack, v_stack, cache_k, cache_v, bias_c, bias_n, l, B, T, qcol):
    H = A_HEADS
    Lbuf = cache_k.shape[2] // H
    npat = len(PATTERNS)
    kern = functools.partial(_attn_s_kernel, Lbuf=Lbuf)
    return pl.pallas_call(
        kern,
        grid=(B,),
        in_specs=[
            pl.BlockSpec((T, A_WIDTH), lambda b: (b, qcol // A_WIDTH)),
            pl.BlockSpec((None, T, A_WIDTH), lambda b: (l, b, 0)),
            pl.BlockSpec((None, T, A_WIDTH), lambda b: (l, b, 0)),
            pl.BlockSpec((None, None, Lbuf * H, A_DH), lambda b: (l, b, 0, 0)),
            pl.BlockSpec((None, None, Lbuf * H, A_DH), lambda b: (l, b, 0, 0)),
            pl.BlockSpec((H, npat, T, Lbuf), lambda b: (0, 0, 0, 0)),
            pl.BlockSpec((H, npat, T, T), lambda b: (0, 0, 0, 0)),
        ],
        out_specs=pl.BlockSpec((T, A_WIDTH), lambda b: (b, 0)),
        out_shape=jax.ShapeDtypeStruct((B * T, A_WIDTH), F32),
        compiler_params=_params(("parallel",)),
        name="attn_sample",
    )(proj, k_stack, v_stack, cache_k, cache_v, bias_c, bias_n)


def _outproj_kernel(x_ref, hm_ref, ha_ref, gm_ref, ga_ref, wpm_ref, wpa_ref, wo_ref, o_ref):
    pm = _dot(hm_ref[...].astype(BF16), wpm_ref[...])
    pa = _dot(ha_ref[...].astype(BF16), wpa_ref[...])
    merged = jax.nn.sigmoid(gm_ref[...]) * pm + jax.nn.sigmoid(ga_ref[...]) * pa
    o_ref[...] = x_ref[...] + _dot(merged.astype(BF16), wo_ref[...])


def _outproj(x, hm, ha, proj, w_pm, w_pa, w_out, l, tm):
    M, D = x.shape
    once = pl.Buffered(1)
    return pl.pallas_call(
        _outproj_kernel,
        grid=(M // tm,),
        in_specs=[
            pl.BlockSpec((tm, D), lambda i: (i, 0)),
            pl.BlockSpec((tm, M_WIDTH), lambda i: (i, 0)),
            pl.BlockSpec((tm, A_WIDTH), lambda i: (i, 0)),
            pl.BlockSpec((tm, D), lambda i: (i, 0)),
            pl.BlockSpec((tm, D), lambda i: (i, 1)),
            pl.BlockSpec((None, M_WIDTH, D), lambda i: (l, 0, 0), pipeline_mode=once),
            pl.BlockSpec((None, A_WIDTH, D), lambda i: (l, 0, 0), pipeline_mode=once),
            pl.BlockSpec((None, D, D), lambda i: (l, 0, 0), pipeline_mode=once),
        ],
        out_specs=pl.BlockSpec((tm, D), lambda i: (i, 0)),
        out_shape=jax.ShapeDtypeStruct((M, D), F32),
        compiler_params=_params(("parallel",)),
        name="outproj",
    )(x, hm, ha, proj, proj, w_pm, w_pa, w_out)


def _norm_kernel(x_ref, g_ref, o_ref):
    o_ref[...] = _rms(x_ref[...], g_ref[...])


def _final_norm(x, g, tm):
    M, D = x.shape
    return pl.pallas_call(
        _norm_kernel,
        grid=(M // tm,),
        in_specs=[pl.BlockSpec((tm, D), lambda i: (i, 0)), pl.BlockSpec((1, D), lambda i: (0, 0))],
        out_specs=pl.BlockSpec((tm, D), lambda i: (i, 0)),
        out_shape=jax.ShapeDtypeStruct((M, D), F32),
        compiler_params=_params(("parallel",)),
        name="final_norm",
    )(x, g)


def _t5_bucket(dist):
    exact = N_BUCKETS // 2
    d32 = jnp.maximum(dist, 1).astype(F32)
    large = exact + (jnp.log(d32 / exact) / math.log(MAX_DISTANCE / exact) * (N_BUCKETS - exact)).astype(jnp.int32)
    large = jnp.minimum(large, N_BUCKETS - 1)
    return jnp.where(dist < exact, dist, large)


def _step_biases(rel_table):
    assert all(w // d == N_STEPS for w, d in PATTERNS)
    buckets = jnp.stack([_t5_bucket(d * jnp.arange(N_STEPS + 1, dtype=jnp.int32)) for (w, d) in PATTERNS])
    onehot = (buckets[:, :, None] == jnp.arange(N_BUCKETS)[None, None, :]).astype(F32)
    return jnp.einsum("pjn,nh->phj", onehot, rel_table.astype(F32), precision=lax.Precision.HIGHEST)


def _toeplitz(g, rows, cols, off):
    n = g.shape[-1]
    width = off + cols
    assert width <= n - 1 and off - (rows - 1) >= 0
    flat = jnp.tile(g, (1,) * (g.ndim - 1) + (rows,))[..., :rows * (n - 1)]
    return flat.reshape(g.shape[:-1] + (rows, n - 1))[..., off:off + cols]


def _prompt_bias(sb):
    Q = Q_BLOCK
    ninf = lambda n: jnp.full(sb.shape[:-1] + (n,), -jnp.inf, F32)
    g = jnp.concatenate([ninf(Q - 1), sb[..., ::-1], ninf(Q)], axis=-1)
    base = _toeplitz(g, Q, 2 * Q, Q - 1)
    first = jnp.where(jnp.arange(2 * Q) >= Q, base, -jnp.inf)
    return jnp.transpose(jnp.stack([base, first], axis=2), (1, 0, 2, 3, 4))


def _sample_bias(sb, T, Lbuf):
    n = Lbuf + T
    tabs = []
    for p, (w, d) in enumerate(PATTERNS):
        bd = jnp.concatenate([sb[p][..., None], jnp.full(sb[p].shape + (d - 1,), -jnp.inf, F32)], axis=-1)
        bd = bd.reshape(sb.shape[1], -1)
        bd = jnp.concatenate([bd, jnp.full((sb.shape[1], n), -jnp.inf, F32)], axis=-1)[:, :n]
        g = jnp.concatenate([bd[:, ::-1], jnp.full((sb.shape[1], T), -jnp.inf, F32)], axis=-1)
        tabs.append(_toeplitz(g, T, n, T - 1))
    tab = jnp.stack(tabs, axis=1)
    return tab[..., :Lbuf], tab[..., Lbuf:]


def _row_tile(M, cap):
    return cap if M % cap == 0 else M


def _trunk(x3, wts, sb, states):
    B, T, D = x3.shape
    M = B * T
    depth = wts["w_ab"].shape[0]
    prompt = states is None
    FF = wts["w_ffa_out"].shape[1]
    tm = _row_tile(M, 1024)
    tf = 512 if FF % 512 == 0 else FF
    tn = W_TILE
    tm_out = _row_tile(M, 512)
    col_m = 2 * D
    col_q = 2 * D + QK_WIDTH + 2 * M_WIDTH
    if prompt:
        bias = _prompt_bias(sb)
        cbuf = jnp.zeros((1, B, CONV_PAD, QK_WIDTH), F32)
        C0 = jnp.zeros((1, B, M_HEADS, M_DK, M_DV), F32)
        n0 = jnp.zeros((1, B, M_HEADS, 1, M_DK), F32)
        m0 = jnp.zeros((1, B, M_HEADS, 1, 1), F32)
    else:
        cache_k, cache_v, sC, sn, sm, sconv = states
        Lbuf = cache_k.shape[2]
        bias_c, bias_n = _sample_bias(sb, T, Lbuf)
        cache_k = cache_k.astype(F32).reshape(depth, B, Lbuf * A_HEADS, A_DH)
        cache_v = cache_v.astype(F32).reshape(depth, B, Lbuf * A_HEADS, A_DH)
        cbuf = jnp.pad(sconv.astype(F32), ((0, 0), (0, 0), (CONV_PAD - (CONV_W - 1), 0), (0, 0)))
        C0 = sC.astype(F32)
        n0 = sn.astype(F32).reshape(depth, B, M_HEADS, 1, M_DK)
        m0 = sm.astype(F32).reshape(depth, B, M_HEADS, 1, 1)

    x = x3.reshape(M, D)
    k_stack = jnp.zeros((depth, M, A_WIDTH), F32)
    v_stack = jnp.zeros((depth, M, A_WIDTH), F32)
    Cs, ns, ms, convs = [], [], [], []
    for l in range(depth):
        x = _ffn(x, wts["ln_ffa"], wts["w_ffa_in"], wts["w_ffa_out"], l, tm, tf)
        proj, gif = _inproj(x, wts["ln_mix"], wts["w_ab"], wts["w_if"], l, tm, tn)
        k_stack, v_stack = _kvproj(x, wts["ln_mix"], wts["w_ab"], k_stack, v_stack, l, tm, tn)
        ls = 0 if prompt else l
        hm, C1, n1, m1 = _mlstm(proj, gif, wts["b_if"], wts["conv_w"], wts["conv_b"], wts["m_norm"],
                                cbuf, C0, n0, m0, l, ls, B, T, col_m,
                                BF16 if prompt else F32, BF16 if prompt else F32)
        if prompt:
            ha = _attn_prompt(proj, k_stack, v_stack, bias, l, B, T, col_q)
        else:
            ha = _attn_sample(proj, k_stack, v_stack, cache_k, cache_v, bias_c, bias_n, l, B, T, col_q)
        x = _outproj(x, hm, ha, proj, wts["w_pm"], wts["w_pa"], wts["w_out"], l, tm_out)
        x = _ffn(x, wts["ln_ffb"], wts["w_ffb_in"], wts["w_ffb_out"], l, tm, tf)
        Cs.append(C1)
        ns.append(n1.reshape(B, M_HEADS, M_DK))
        ms.append(m1.reshape(B, M_HEADS))
        pre = proj.reshape(B, T, -1)[:, :, col_m:col_m + QK_WIDTH]
        if T >= CONV_W - 1:
            convs.append(pre[:, T - (CONV_W - 1):])
        else:
            convs.append(jnp.concatenate([cbuf[ls][:, CONV_PAD - (CONV_W - 1) + T:], pre], axis=1))
    y = _final_norm(x, wts["ln_f"], tm).reshape(B, T, D)
    stk = lambda xs: jnp.stack(xs, axis=0)
    keep = min(MAX_WINDOW, T)
    k_new = k_stack.reshape(depth, B, T, A_HEADS, A_DH)[:, :, T - keep:]
    v_new = v_stack.reshape(depth, B, T, A_HEADS, A_DH)[:, :, T - keep:]
    return y, (k_new, v_new, stk(Cs), stk(ns), stk(ms), stk(convs))


def kernel(x_prompt, x_sample, cache_k, cache_v, state_C, state_n, state_m, state_conv, w_in, conv_w, conv_b, b_if, m_norm, w_pm, w_pa, w_out, rel_table, ln_ffa, w_ffa_in, w_ffa_out, ln_mix, ln_ffb, w_ffb_in, w_ffb_out, ln_f):
    depth, D, _ = w_in.shape
    o_if = QK_WIDTH + 2 * M_WIDTH
    o_a = o_if + 2 * M_HEADS
    w_t = jnp.swapaxes(w_in, 1, 2)
    w_if32 = jnp.pad(w_t[:, o_if:o_a, :], ((0, 0), (0, LANE - 2 * M_HEADS), (0, 0)))
    w_if_hi = w_if32.astype(BF16)
    w_if_lo = (w_if32 - w_if_hi.astype(F32)).astype(BF16)
    wts = dict(
        w_ab=_wprep(w_t, o_if, o_a - o_if, W_TILE),
        w_if=jnp.stack([w_if_hi, w_if_lo], axis=1),
        b_if=jnp.pad(b_if.astype(F32), ((0, 0), (0, LANE - 2 * M_HEADS))).reshape(depth, 1, LANE),
        conv_w=conv_w.astype(F32),
        conv_b=conv_b.astype(F32).reshape(depth, 1, QK_WIDTH),
        m_norm=m_norm.astype(F32).reshape(depth, 1, M_WIDTH),
        w_pm=w_pm.astype(BF16), w_pa=w_pa.astype(BF16), w_out=w_out.astype(BF16),
        ln_ffa=ln_ffa.astype(F32).reshape(depth, 1, D), ln_mix=ln_mix.astype(F32).reshape(depth, 1, D),
        ln_ffb=ln_ffb.astype(F32).reshape(depth, 1, D), ln_f=ln_f.astype(F32).reshape(1, D),
        w_ffa_in=w_ffa_in.astype(BF16), w_ffa_out=w_ffa_out.astype(BF16),
        w_ffb_in=w_ffb_in.astype(BF16), w_ffb_out=w_ffb_out.astype(BF16),
    )
    sb = _step_biases(rel_table)
    y_p, (k_p, v_p, C_p, n_p, m_p, conv_p) = _trunk(x_prompt, wts, sb, None)
    y_s, (k_s, v_s, C_s, n_s, m_s, conv_s) = _trunk(
        x_sample, wts, sb, (cache_k, cache_v, state_C, state_n, state_m, state_conv))
    return (y_p, y_s, k_p, v_p, C_p, n_p, m_p, conv_p, k_s, v_s, C_s, n_s, m_s, conv_s)
```

```python
import functools
import math

import jax
import jax.numpy as jnp
from jax import lax
from jax.experimental import pallas as pl
from jax.experimental.pallas import tpu as pltpu

F32 = jnp.float32
BF16 = jnp.bfloat16

M_HEADS = 4
M_DK = 256
M_DV = 256
M_WIDTH = M_HEADS * M_DV
QK_WIDTH = 2 * M_HEADS * M_DK
CONV_W = 4
CHUNK = 128
A_HEADS = 8
A_DH = 128
A_WIDTH = A_HEADS * A_DH
PATTERNS = ((128, 1), (512, 4), (2048, 16))
MAX_WINDOW = 2048
Q_BLOCK = 128
N_BUCKETS = 32
MAX_DISTANCE = MAX_WINDOW
EPS = 1e-6
N_STEPS = 128
LANE = 128
CONV_PAD = 8
VMEM_LIMIT = 58 * 1024 * 1024


def _dot(a, b):
    return jnp.dot(a, b, preferred_element_type=F32)


def _dot_nt(a, b):
    return lax.dot_general(a, b, (((1,), (1,)), ((), ())), preferred_element_type=F32)


def _dot_tn(a, b):
    return lax.dot_general(a, b, (((0,), (0,)), ((), ())), preferred_element_type=F32)


def _rms(x, g):
    return x * lax.rsqrt(jnp.mean(x * x, axis=-1, keepdims=True) + EPS) * g


def _params(sem):
    return pltpu.CompilerParams(dimension_semantics=sem, vmem_limit_bytes=VMEM_LIMIT)


def _ffn_kernel(x_ref, xs_ref, g_ref, wg_ref, wu_ref, wo_ref, o_ref, os_ref, h_ref):
    i = pl.program_id(0)
    j = pl.program_id(1)
    last = pl.num_programs(1) - 1
    tm = x_ref.shape[0]
    ts = xs_ref.shape[0]

    @pl.when(j == 0)
    def _():
        h_ref[0:tm, :] = _rms(x_ref[...], g_ref[...]).astype(BF16)
        o_ref[...] = jnp.zeros_like(o_ref)

    @pl.when((i == 0) & (j == 0))
    def _():
        h_ref[tm:tm + ts, :] = _rms(xs_ref[...], g_ref[...]).astype(BF16)
        os_ref[...] = jnp.zeros_like(os_ref)

    def swiglu(rows):
        h = h_ref[0:rows, :]
        gate = _dot(h, wg_ref[...])
        up = _dot(h, wu_ref[...])
        act = (gate * jax.nn.sigmoid(gate) * up).astype(BF16)
        return _dot(act, wo_ref[...])

    @pl.when(i == 0)
    def _():
        res = swiglu(tm + ts)
        o_ref[...] += res[0:tm, :]
        os_ref[...] += res[tm:tm + ts, :]

    @pl.when(i > 0)
    def _():
        o_ref[...] += swiglu(tm)

    @pl.when(j == last)
    def _():
        o_ref[...] = x_ref[...] + 0.5 * o_ref[...]

    @pl.when((i == 0) & (j == last))
    def _():
        os_ref[...] = xs_ref[...] + 0.5 * os_ref[...]


def _ffn(x, xs, g, w_in, w_out, l, tm, tf):
    M, D = x.shape
    Ms = xs.shape[0]
    FF = w_out.shape[1]
    nj = FF // tf
    return pl.pallas_call(
        _ffn_kernel,
        grid=(M // tm, nj),
        in_specs=[
            pl.BlockSpec((tm, D), lambda i, j: (i, 0)),
            pl.BlockSpec((Ms, D), lambda i, j: (0, 0)),
            pl.BlockSpec((None, 1, D), lambda i, j: (l, 0, 0)),
            pl.BlockSpec((None, D, tf), lambda i, j: (l, 0, j)),
            pl.BlockSpec((None, D, tf), lambda i, j: (l, 0, j + nj)),
            pl.BlockSpec((None, tf, D), lambda i, j: (l, j, 0)),
        ],
        out_specs=[
            pl.BlockSpec((tm, D), lambda i, j: (i, 0)),
            pl.BlockSpec((Ms, D), lambda i, j: (0, 0)),
        ],
        out_shape=[jax.ShapeDtypeStruct((M, D), F32), jax.ShapeDtypeStruct((Ms, D), F32)],
        scratch_shapes=[pltpu.VMEM((tm + Ms, D), BF16)],
        compiler_params=_params(("arbitrary", "arbitrary")),
        name="ffn",
    )(x, xs, g, w_in, w_in, w_out)


W_TILE = 1024
SUBLANE = 8


def _wprep_kernel(w1_ref, w2_ref, o_ref, *, nA, shift):
    j = pl.program_id(1)

    @pl.when(j < nA)
    def _():
        o_ref[...] = w1_ref[...].astype(BF16)

    @pl.when(j >= nA)
    def _():
        o_ref[...] = jnp.concatenate([w1_ref[shift:, :], w2_ref[...]], axis=0).astype(BF16)


def _wprep(w_t, n_a, shift, tn):
    depth, n_in, D = w_t.shape
    assert n_a % tn == 0 and (n_in - shift) % tn == 0 and shift == SUBLANE
    nj = (n_in - shift) // tn
    kern = functools.partial(_wprep_kernel, nA=n_a // tn, shift=shift)
    return pl.pallas_call(
        kern,
        grid=(depth, nj),
        in_specs=[
            pl.BlockSpec((None, tn, D), lambda l, j: (l, j, 0)),
            pl.BlockSpec((None, shift, D), lambda l, j: (l, (j + 1) * (tn // shift), 0)),
        ],
        out_specs=pl.BlockSpec((None, tn, D), lambda l, j: (l, j, 0)),
        out_shape=jax.ShapeDtypeStruct((depth, n_in - shift, D), BF16),
        compiler_params=_params(("parallel", "parallel")),
        name="wprep",
    )(w_t, w_t)


def _inproj_kernel(x_ref, g_ref, w_ref, wif_ref, o_ref, gif_ref, h_ref):
    j = pl.program_id(1)

    @pl.when(j == 0)
    def _():
        h32 = _rms(x_ref[...], g_ref[...])
        hb = h32.astype(BF16)
        h_ref[...] = hb
        hl = (h32 - hb.astype(F32)).astype(BF16)
        gif_ref[...] = _dot_nt(hb, wif_ref[0]) + _dot_nt(hb, wif_ref[1]) + _dot_nt(hl, wif_ref[0])

    o_ref[...] = _dot_nt(h_ref[...], w_ref[...])


def _inproj(x, g, w_ab, w_if, l, tm, tn):
    M, D = x.shape
    nG, nA, nQ = 2 * D // tn, (QK_WIDTH + 2 * M_WIDTH) // tn, A_WIDTH // tn
    NP = 2 * D + QK_WIDTH + 2 * M_WIDTH + A_WIDTH
    w_idx = lambda j: jnp.where(j < nG, j + nA + 3 * nQ, j - nG)
    return pl.pallas_call(
        _inproj_kernel,
        grid=(M // tm, NP // tn),
        in_specs=[
            pl.BlockSpec((tm, D), lambda i, j: (i, 0)),
            pl.BlockSpec((None, 1, D), lambda i, j: (l, 0, 0)),
            pl.BlockSpec((None, tn, D), lambda i, j: (l, w_idx(j), 0)),
            pl.BlockSpec((None, 2, LANE, D), lambda i, j: (l, 0, 0, 0)),
        ],
        out_specs=[
            pl.BlockSpec((tm, tn), lambda i, j: (i, j)),
            pl.BlockSpec((tm, LANE), lambda i, j: (i, 0)),
        ],
        out_shape=[jax.ShapeDtypeStruct((M, NP), F32), jax.ShapeDtypeStruct((M, LANE), F32)],
        scratch_shapes=[pltpu.VMEM((tm, D), BF16)],
        compiler_params=_params(("parallel", "arbitrary")),
        name="inproj",
    )(x, g, w_ab, w_if)


def _kvproj_kernel(x_ref, g_ref, w_ref, *refs, nQ):
    k_ref, v_ref, h_ref = refs[-3:]
    j = pl.program_id(1)

    @pl.when(j == 0)
    def _():
        h_ref[...] = _rms(x_ref[...], g_ref[...]).astype(BF16)

    @pl.when(j < nQ)
    def _():
        k_ref[...] = _dot_nt(h_ref[...], w_ref[...])

    @pl.when(j >= nQ)
    def _():
        v_ref[...] = _dot_nt(h_ref[...], w_ref[...])


def _kvproj(x, g, w_ab, stacks, l, tm, tn):
    M, D = x.shape
    depth = w_ab.shape[0]
    nA, nQ = (QK_WIDTH + 2 * M_WIDTH) // tn, A_WIDTH // tn
    kern = functools.partial(_kvproj_kernel, nQ=nQ)
    carried = () if stacks is None else tuple(stacks)
    return pl.pallas_call(
        kern,
        grid=(M // tm, 2 * nQ),
        in_specs=[
            pl.BlockSpec((tm, D), lambda i, j: (i, 0)),
            pl.BlockSpec((None, 1, D), lambda i, j: (l, 0, 0)),
            pl.BlockSpec((None, tn, D), lambda i, j: (l, nA + nQ + j, 0)),
        ] + [pl.BlockSpec(memory_space=pl.ANY) for _ in carried],
        out_specs=[
            pl.BlockSpec((None, tm, tn), lambda i, j: (l, i, jnp.minimum(j, nQ - 1))),
            pl.BlockSpec((None, tm, tn), lambda i, j: (l, i, jnp.maximum(j - nQ, 0))),
        ],
        out_shape=[jax.ShapeDtypeStruct((depth, M, A_WIDTH), F32)] * 2,
        scratch_shapes=[pltpu.VMEM((tm, D), BF16)],
        input_output_aliases={3 + n: n for n in range(len(carried))},
        compiler_params=_params(("parallel", "arbitrary")),
        name="kvproj",
    )(x, g, w_ab, *carried)


def _mlstm_kernel(qp_ref, kp_ref, v_ref, og_ref, gif_ref, bif_ref, cw_ref, cb_ref, mn_ref,
                  cbuf_ref, C0_ref, n0_ref, m0_ref,
                  hm_ref, C_ref, n_ref, m_ref, xq_ref, xk_ref, *, L, mm_dtype):
    c = pl.program_id(1)
    half = QK_WIDTH // 2

    @pl.when(c == 0)
    def _():
        xq_ref[0:CONV_PAD, :] = cbuf_ref[:, 0:half]
        xk_ref[0:CONV_PAD, :] = cbuf_ref[:, half:QK_WIDTH]
        C_ref[...] = C0_ref[...]
        n_ref[...] = n0_ref[...]
        m_ref[...] = m0_ref[...]

    @pl.when(c > 0)
    def _():
        xq_ref[0:CONV_PAD, :] = xq_ref[L:L + CONV_PAD, :]
        xk_ref[0:CONV_PAD, :] = xk_ref[L:L + CONV_PAD, :]

    xq_ref[CONV_PAD:CONV_PAD + L, :] = qp_ref[...]
    xk_ref[CONV_PAD:CONV_PAD + L, :] = kp_ref[...]

    cw = cw_ref[...]
    cb = cb_ref[...]

    def conv_silu(x_ref, w, b):
        xs = x_ref[...]
        y = b + xs * w[CONV_W - 1:CONV_W, :]
        for s in range(1, CONV_W):
            y = y + pltpu.roll(xs, s, axis=0) * w[CONV_W - 1 - s:CONV_W - s, :]
        y = y[CONV_PAD:, :]
        return y * jax.nn.sigmoid(y)

    qa = conv_silu(xq_ref, cw[:, 0:half], cb[:, 0:half])
    ka = conv_silu(xk_ref, cw[:, half:QK_WIDTH], cb[:, half:QK_WIDTH]) * (M_DK ** -0.5)
    g = gif_ref[...] + bif_ref[...]

    ti = lax.broadcasted_iota(jnp.int32, (L, L), 0)
    si = lax.broadcasted_iota(jnp.int32, (L, L), 1)
    eye = ti == si
    low = si <= ti
    upp = ti <= si

    for h in range(M_HEADS):
        sl = slice(h * M_DV, (h + 1) * M_DV)
        ig_col = g[:, h:h + 1]
        gf = g[:, M_HEADS + h:M_HEADS + h + 1]
        lf_col = jnp.minimum(gf, 0.0) - jnp.log1p(jnp.exp(-jnp.abs(gf)))
        ig_row = jnp.sum(jnp.where(eye, ig_col, 0.0), axis=0, keepdims=True)
        lf_row = jnp.sum(jnp.where(eye, lf_col, 0.0), axis=0, keepdims=True)
        F_col = jnp.sum(jnp.where(low, lf_row, 0.0), axis=1, keepdims=True)
        F_row = jnp.sum(jnp.where(upp, lf_col, 0.0), axis=0, keepdims=True)
        m0 = m_ref[h]
        C0 = C_ref[h]
        n0 = n_ref[h]
        logD = jnp.where(low, F_col - F_row + ig_row, -jnp.inf)
        lst = F_col + m0
        m_col = jnp.maximum(lst, jnp.max(logD, axis=1, keepdims=True))
        q = qa[:, sl]
        k = ka[:, sl]
        qb = q.astype(mm_dtype)
        kb = k.astype(mm_dtype)
        vb = v_ref[:, sl].astype(mm_dtype)
        wgt = jnp.exp(logD - m_col) * _dot_nt(qb, kb)
        sc = jnp.exp(lst - m_col)
        num = _dot(wgt.astype(mm_dtype), vb) + sc * _dot(qb, C0.astype(mm_dtype))
        den = jnp.sum(wgt, axis=1, keepdims=True) + sc * jnp.sum(q * n0, axis=1, keepdims=True)
        hh = num / jnp.maximum(jnp.abs(den), jnp.exp(-m_col))
        FT = F_col[L - 1:L, :]
        lT = FT + m0
        m_new = jnp.maximum(lT, jnp.max(FT - F_row + ig_row, axis=1, keepdims=True))
        e_col = jnp.exp(FT - F_col + ig_col - m_new)
        sT = jnp.exp(lT - m_new)
        ke = k * e_col
        C_ref[h] = sT * C0 + _dot_tn(ke.astype(mm_dtype), vb)
        n_ref[h] = sT * n0 + jnp.sum(ke, axis=0, keepdims=True)
        m_ref[h] = m_new
        hn = hh * lax.rsqrt(jnp.mean(hh * hh, axis=1, keepdims=True) + EPS)
        out = jax.nn.sigmoid(og_ref[:, sl]) * hn * mn_ref[:, sl]
        hm_ref[:, sl] = out.astype(hm_ref.dtype)


def _mlstm(proj, gif, b_if, conv_w, conv_b, m_norm, cbuf, C0, n0, m0, l, ls, B, T, col0, out_dtype, mm_dtype):
    L = CHUNK if T % CHUNK == 0 else T
    nc = T // L
    qo = col0 // M_WIDTH
    kern = functools.partial(_mlstm_kernel, L=L, mm_dtype=mm_dtype)
    row = lambda b, c: b * nc + c
    return pl.pallas_call(
        kern,
        grid=(B, nc),
        in_specs=[
            pl.BlockSpec((L, M_WIDTH), lambda b, c: (row(b, c), qo)),
            pl.BlockSpec((L, M_WIDTH), lambda b, c: (row(b, c), qo + 1)),
            pl.BlockSpec((L, M_WIDTH), lambda b, c: (row(b, c), qo + 2)),
            pl.BlockSpec((L, M_WIDTH), lambda b, c: (row(b, c), qo + 3)),
            pl.BlockSpec((L, LANE), lambda b, c: (row(b, c), 0)),
            pl.BlockSpec((None, 1, LANE), lambda b, c: (l, 0, 0)),
            pl.BlockSpec((None, CONV_W, QK_WIDTH), lambda b, c: (l, 0, 0)),
            pl.BlockSpec((None, 1, QK_WIDTH), lambda b, c: (l, 0, 0)),
            pl.BlockSpec((None, 1, M_WIDTH), lambda b, c: (l, 0, 0)),
            pl.BlockSpec((None, None, CONV_PAD, QK_WIDTH), lambda b, c: (ls, b, 0, 0)),
            pl.BlockSpec((None, None, M_HEADS, M_DK, M_DV), lambda b, c: (ls, b, 0, 0, 0)),
            pl.BlockSpec((None, None, M_HEADS, 1, M_DK), lambda b, c: (ls, b, 0, 0, 0)),
            pl.BlockSpec((None, None, M_HEADS, 1, 1), lambda b, c: (ls, b, 0, 0, 0)),
        ],
        out_specs=[
            pl.BlockSpec((L, M_WIDTH), lambda b, c: (row(b, c), 0)),
            pl.BlockSpec((None, M_HEADS, M_DK, M_DV), lambda b, c: (b, 0, 0, 0)),
            pl.BlockSpec((None, M_HEADS, 1, M_DK), lambda b, c: (b, 0, 0, 0)),
            pl.BlockSpec((None, M_HEADS, 1, 1), lambda b, c: (b, 0, 0, 0)),
        ],
        out_shape=[
            jax.ShapeDtypeStruct((B * T, M_WIDTH), out_dtype),
            jax.ShapeDtypeStruct((B, M_HEADS, M_DK, M_DV), F32),
            jax.ShapeDtypeStruct((B, M_HEADS, 1, M_DK), F32),
            jax.ShapeDtypeStruct((B, M_HEADS, 1, 1), F32),
        ],
        scratch_shapes=[pltpu.VMEM((L + CONV_PAD, M_WIDTH), F32), pltpu.VMEM((L + CONV_PAD, M_WIDTH), F32)],
        compiler_params=_params(("parallel", "arbitrary")),
        name="mlstm",
    )(proj, proj, proj, proj, gif, b_if, conv_w, conv_b, m_norm, cbuf, C0, n0, m0)


STRIDE = 4


def _attn_kernel(q_ref, k_ref, v_ref, bias_ref, o_ref,
                 p4, p16, qbs, kbs, vbs, ops, lps, o4, l4, on, ln, *, S):
    scale = A_DH ** -0.5
    QB = Q_BLOCK
    nunits = S // QB
    G = S // STRIDE
    srcs = (q_ref, k_ref, v_ref)

    def load_operands(p, get):
        qb, kb, vb = qbs.at[p], kbs.at[p], vbs.at[p]
        kb[0:QB, :] = jnp.zeros((QB, A_DH), BF16)
        vb[0:QB, 0:A_DH] = jnp.zeros((QB, A_DH), BF16)
        vb[:, A_DH:2 * A_DH] = jnp.ones((S + QB, A_DH), BF16)
        qb[...] = (get(0) * scale).astype(BF16)
        kb[QB:QB + S, :] = get(1).astype(BF16)
        vb[QB:QB + S, 0:A_DH] = get(2).astype(BF16)

    def unit(u, p, nblk, dst_o, dst_l):
        qb, kb, vb = qbs.at[p], kbs.at[p], vbs.at[p]
        r0 = u * QB
        qv = qb[r0:r0 + QB, :]
        if nblk == 1:
            kw = kb[QB + r0:2 * QB + r0, :]
            vw = vb[QB + r0:2 * QB + r0, :]
            bias = bias_ref[p, 0, :, QB:2 * QB]
        else:
            kw = kb[r0:r0 + 2 * QB, :]
            vw = vb[r0:r0 + 2 * QB, :]
            bias = bias_ref[p, 1 if u % nblk == 0 else 0]
        s = _dot_nt(qv, kw) + bias
        m = jnp.max(s, axis=1, keepdims=True)
        pv = _dot(jnp.exp(s - m).astype(BF16), vw)
        den = pv[:, A_DH:2 * A_DH]
        dst_o[r0:r0 + QB, :] = pv[:, 0:A_DH] / den
        dst_l[r0:r0 + QB, :] = m + jnp.log(den)

    load_operands(0, lambda a: srcs[a][...])
    for u in range(nunits):
        unit(u, 0, nunits, on.at[0], ln.at[0])

    for a in range(3):
        for r in range(STRIDE):
            p4[a, r * G:(r + 1) * G, :] = srcs[a][pl.ds(r, G, stride=STRIDE), :]
    load_operands(1, lambda a: p4[a])
    for u in range(nunits):
        unit(u, 1, G // QB, ops.at[0], lps.at[0])
    for r in range(STRIDE):
        on[1, pl.ds(r, G, stride=STRIDE), :] = ops[0, r * G:(r + 1) * G, :]
        ln[1, pl.ds(r, G, stride=STRIDE), :] = lps[0, r * G:(r + 1) * G, :]

    for a in range(3):
        for r in range(STRIDE):
            for c in range(STRIDE):
                u = r * STRIDE + c
                p16[a, u * QB:(u + 1) * QB, :] = p4[a, pl.ds(r * G + c, QB, stride=STRIDE), :]
    load_operands(2, lambda a: p16[a])
    for u in range(nunits):
        unit(u, 2, 1, ops.at[1], lps.at[1])
    for r in range(STRIDE):
        for c in range(STRIDE):
            u = r * STRIDE + c
            o4[pl.ds(r * G + c, QB, stride=STRIDE), :] = ops[1, u * QB:(u + 1) * QB, :]
            l4[pl.ds(r * G + c, QB, stride=STRIDE), :] = lps[1, u * QB:(u + 1) * QB, :]
    for r in range(STRIDE):
        on[2, pl.ds(r, G, stride=STRIDE), :] = o4[r * G:(r + 1) * G, :]
        ln[2, pl.ds(r, G, stride=STRIDE), :] = l4[r * G:(r + 1) * G, :]

    def combine(bi, carry):
        rows = pl.ds(pl.multiple_of(bi * QB, QB), QB)
        ls = [ln[p, rows, :] for p in range(len(PATTERNS))]
        lmax = functools.reduce(jnp.maximum, ls)
        tot = jnp.zeros((QB, A_DH), F32)
        wsum = jnp.zeros((QB, A_DH), F32)
        for p in range(len(PATTERNS)):
            a = jnp.exp(ls[p] - lmax)
            tot = tot + a * on[p, rows, :]
            wsum = wsum + a
        o_ref[rows, :] = (tot / wsum).astype(o_ref.dtype)
        return carry

    lax.fori_loop(0, nunits, combine, 0)


def _attn_prompt(proj, k_stack, v_stack, bias, l, B, S, qcol):
    assert S == N_STEPS * PATTERNS[-1][1] and PATTERNS[1][1] == STRIDE and PATTERNS[2][1] == STRIDE * STRIDE
    H = A_HEADS
    cb = qcol // A_DH
    kern = functools.partial(_attn_kernel, S=S)
    npat = len(PATTERNS)
    return pl.pallas_call(
        kern,
        grid=(B, H),
        in_specs=[
            pl.BlockSpec((S, A_DH), lambda b, h: (b, cb + h)),
            pl.BlockSpec((None, S, A_DH), lambda b, h: (l, b, h)),
            pl.BlockSpec((None, S, A_DH), lambda b, h: (l, b, h)),
            pl.BlockSpec((None, npat, 2, Q_BLOCK, 2 * Q_BLOCK), lambda b, h: (h, 0, 0, 0, 0)),
        ],
        out_specs=pl.BlockSpec((S, A_DH), lambda b, h: (b, h)),
        out_shape=jax.ShapeDtypeStruct((B * S, A_WIDTH), BF16),
        scratch_shapes=[
            pltpu.VMEM((3, S, A_DH), F32),
            pltpu.VMEM((3, S, A_DH), F32),
            pltpu.VMEM((npat, S, A_DH), BF16),
            pltpu.VMEM((npat, S + Q_BLOCK, A_DH), BF16),
            pltpu.VMEM((npat, S + Q_BLOCK, 2 * A_DH), BF16),
            pltpu.VMEM((2, S, A_DH), F32),
            pltpu.VMEM((2, S, A_DH), F32),
            pltpu.VMEM((S, A_DH), F32),
            pltpu.VMEM((S, A_DH), F32),
            pltpu.VMEM((npat, S, A_DH), F32),
            pltpu.VMEM((npat, S, A_DH), F32),
        ],
        compiler_params=_params(("parallel", "parallel")),
        name="attn_prompt",
    )(proj, k_stack, v_stack, bias)


def _attn_s_kernel(q_ref, kn_ref, vn_ref, kc_ref, vc_ref, bc_ref, bn_ref, o_ref, *, Lbuf):
    H = A_HEADS
    npat = len(PATTERNS)
    for h in range(H):
        hs = slice(h * A_DH, (h + 1) * A_DH)
        q = q_ref[:, hs] * (A_DH ** -0.5)
        kc = kc_ref[pl.ds(h, Lbuf, stride=H), :]
        vc = vc_ref[pl.ds(h, Lbuf, stride=H), :]
        s_c = _dot_nt(q, kc)
        s_n = _dot_nt(q, kn_ref[:, hs])
        mx = None
        for p in range(npat):
            mp = jnp.maximum(jnp.max(s_c + bc_ref[h, p], axis=1, keepdims=True),
                             jnp.max(s_n + bn_ref[h, p], axis=1, keepdims=True))
            mx = mp if mx is None else jnp.maximum(mx, mp)
        e_c = jnp.exp(s_c + bc_ref[h, 0] - mx)
        e_n = jnp.exp(s_n + bn_ref[h, 0] - mx)
        for p in range(1, npat):
            e_c = e_c + jnp.exp(s_c + bc_ref[h, p] - mx)
            e_n = e_n + jnp.exp(s_n + bn_ref[h, p] - mx)
        num = _dot(e_c, vc) + _dot(e_n, vn_ref[:, hs])
        den = jnp.sum(e_c, axis=1, keepdims=True) + jnp.sum(e_n, axis=1, keepdims=True)
        o_ref[:, hs] = num / den


def _attn_sample(proj, k_stack, v_stack, cache_k, cache_v, bias_c, bias_n, l, B, T, qcol):
    H = A_HEADS
    Lbuf = cache_k.shape[2] // H
    npat = len(PATTERNS)
    kern = functools.partial(_attn_s_kernel, Lbuf=Lbuf)
    return pl.pallas_call(
        kern,
        grid=(B,),
        in_specs=[
            pl.BlockSpec((T, A_WIDTH), lambda b: (b, qcol // A_WIDTH)),
            pl.BlockSpec((None, T, A_WIDTH), lambda b: (l, b, 0)),
            pl.BlockSpec((None, T, A_WIDTH), lambda b: (l, b, 0)),
            pl.BlockSpec((None, None, Lbuf * H, A_DH), lambda b: (l, b, 0, 0)),
            pl.BlockSpec((None, None, Lbuf * H, A_DH), lambda b: (l, b, 0, 0)),
            pl.BlockSpec((H, npat, T, Lbuf), lambda b: (0, 0, 0, 0)),
            pl.BlockSpec((H, npat, T, T), lambda b: (0, 0, 0, 0)),
        ],
        out_specs=pl.BlockSpec((T, A_WIDTH), lambda b: (b, 0)),
        out_shape=jax.ShapeDtypeStruct((B * T, A_WIDTH), F32),
        compiler_params=_params(("parallel",)),
        name="attn_sample",
    )(proj, k_stack, v_stack, cache_k, cache_v, bias_c, bias_n)


def _outproj_kernel(x_ref, hm_ref, ha_ref, gm_ref, ga_ref, wpm_ref, wpa_ref, wo_ref, o_ref):
    pm = _dot(hm_ref[...].astype(BF16), wpm_ref[...])
    pa = _dot(ha_ref[...].astype(BF16), wpa_ref[...])
    merged = jax.nn.sigmoid(gm_ref[...]) * pm + jax.nn.sigmoid(ga_ref[...]) * pa
    o_ref[...] = x_ref[...] + _dot(merged.astype(BF16), wo_ref[...])


def _outproj(x, hm, ha, proj, w_pm, w_pa, w_out, l, tm):
    M, D = x.shape
    once = pl.Buffered(1)
    return pl.pallas_call(
        _outproj_kernel,
        grid=(M // tm,),
        in_specs=[
            pl.BlockSpec((tm, D), lambda i: (i, 0)),
            pl.BlockSpec((tm, M_WIDTH), lambda i: (i, 0)),
            pl.BlockSpec((tm, A_WIDTH), lambda i: (i, 0)),
            pl.BlockSpec((tm, D), lambda i: (i, 0)),
            pl.BlockSpec((tm, D), lambda i: (i, 1)),
            pl.BlockSpec((None, M_WIDTH, D), lambda i: (l, 0, 0), pipeline_mode=once),
            pl.BlockSpec((None, A_WIDTH, D), lambda i: (l, 0, 0), pipeline_mode=once),
            pl.BlockSpec((None, D, D), lambda i: (l, 0, 0), pipeline_mode=once),
        ],
        out_specs=pl.BlockSpec((tm, D), lambda i: (i, 0)),
        out_shape=jax.ShapeDtypeStruct((M, D), F32),
        compiler_params=_params(("parallel",)),
        name="outproj",
    )(x, hm, ha, proj, proj, w_pm, w_pa, w_out)


def _norm_kernel(x_ref, g_ref, o_ref):
    o_ref[...] = _rms(x_ref[...], g_ref[...])


def _final_norm(x, g, tm):
    M, D = x.shape
    return pl.pallas_call(
        _norm_kernel,
        grid=(M // tm,),
        in_specs=[pl.BlockSpec((tm, D), lambda i: (i, 0)), pl.BlockSpec((1, D), lambda i: (0, 0))],
        out_specs=pl.BlockSpec((tm, D), lambda i: (i, 0)),
        out_shape=jax.ShapeDtypeStruct((M, D), F32),
        compiler_params=_params(("parallel",)),
        name="final_norm",
    )(x, g)


def _t5_bucket(dist):
    exact = N_BUCKETS // 2
    d32 = jnp.maximum(dist, 1).astype(F32)
    large = exact + (jnp.log(d32 / exact) / math.log(MAX_DISTANCE / exact) * (N_BUCKETS - exact)).astype(jnp.int32)
    large = jnp.minimum(large, N_BUCKETS - 1)
    return jnp.where(dist < exact, dist, large)


def _step_biases(rel_table):
    assert all(w // d == N_STEPS for w, d in PATTERNS)
    buckets = jnp.stack([_t5_bucket(d * jnp.arange(N_STEPS + 1, dtype=jnp.int32)) for (w, d) in PATTERNS])
    onehot = (buckets[:, :, None] == jnp.arange(N_BUCKETS)[None, None, :]).astype(F32)
    return jnp.einsum("pjn,nh->phj", onehot, rel_table.astype(F32), precision=lax.Precision.HIGHEST)


def _toeplitz(g, rows, cols, off):
    n = g.shape[-1]
    width = off + cols
    assert width <= n - 1 and off - (rows - 1) >= 0
    flat = jnp.tile(g, (1,) * (g.ndim - 1) + (rows,))[..., :rows * (n - 1)]
    return flat.reshape(g.shape[:-1] + (rows, n - 1))[..., off:off + cols]


def _prompt_bias(sb):
    Q = Q_BLOCK
    ninf = lambda n: jnp.full(sb.shape[:-1] + (n,), -jnp.inf, F32)
    g = jnp.concatenate([ninf(Q - 1), sb[..., ::-1], ninf(Q)], axis=-1)
    base = _toeplitz(g, Q, 2 * Q, Q - 1)
    first = jnp.where(jnp.arange(2 * Q) >= Q, base, -jnp.inf)
    return jnp.transpose(jnp.stack([base, first], axis=2), (1, 0, 2, 3, 4))


def _sample_bias(sb, T, Lbuf):
    n = Lbuf + T
    tabs = []
    for p, (w, d) in enumerate(PATTERNS):
        bd = jnp.concatenate([sb[p][..., None], jnp.full(sb[p].shape + (d - 1,), -jnp.inf, F32)], axis=-1)
        bd = bd.reshape(sb.shape[1], -1)
        bd = jnp.concatenate([bd, jnp.full((sb.shape[1], n), -jnp.inf, F32)], axis=-1)[:, :n]
        g = jnp.concatenate([bd[:, ::-1], jnp.full((sb.shape[1], T), -jnp.inf, F32)], axis=-1)
        tabs.append(_toeplitz(g, T, n, T - 1))
    tab = jnp.stack(tabs, axis=1)
    return tab[..., :Lbuf], tab[..., Lbuf:]


def _row_tile(M, cap):
    return cap if M % cap == 0 else M


class _Path:
    def __init__(self, x3, wts, sb, states):
        self.wts = wts
        self.shape = x3.shape
        self.prompt = states is None
        self.depth = wts["w_ab"].shape[0]
        self._setup(x3, sb, states)

    def _setup(self, x3, sb, states):
        B, T, D = self.shape
        wts, depth, prompt = self.wts, self.depth, self.prompt
        M = B * T
        self.tm = _row_tile(M, 1024)
        if prompt:
            self.bias = _prompt_bias(sb)
            self.cbuf = jnp.zeros((1, B, CONV_PAD, QK_WIDTH), F32)
            self.C0 = jnp.zeros((1, B, M_HEADS, M_DK, M_DV), F32)
            self.n0 = jnp.zeros((1, B, M_HEADS, 1, M_DK), F32)
            self.m0 = jnp.zeros((1, B, M_HEADS, 1, 1), F32)
        else:
            cache_k, cache_v, sC, sn, sm, sconv = states
            Lbuf = cache_k.shape[2]
            self.bias_c, self.bias_n = _sample_bias(sb, T, Lbuf)
            self.cache_k = cache_k.astype(F32).reshape(depth, B, Lbuf * A_HEADS, A_DH)
            self.cache_v = cache_v.astype(F32).reshape(depth, B, Lbuf * A_HEADS, A_DH)
            self.cbuf = jnp.pad(sconv.astype(F32), ((0, 0), (0, 0), (CONV_PAD - (CONV_W - 1), 0), (0, 0)))
            self.C0 = sC.astype(F32)
            self.n0 = sn.astype(F32).reshape(depth, B, M_HEADS, 1, M_DK)
            self.m0 = sm.astype(F32).reshape(depth, B, M_HEADS, 1, 1)
        self.x0 = x3.reshape(M, D)
        self.stacks = None
        self.Cs, self.ns, self.ms, self.convs = [], [], [], []

    def mixer(self, x, l):
        B, T, D = self.shape
        wts, prompt, tm = self.wts, self.prompt, self.tm
        col_m = 2 * D
        col_q = 2 * D + QK_WIDTH + 2 * M_WIDTH
        proj, gif = _inproj(x, wts["ln_mix"], wts["w_ab"], wts["w_if"], l, tm, W_TILE)
        self.stacks = _kvproj(x, wts["ln_mix"], wts["w_ab"], self.stacks, l, tm, W_TILE)
        k_stack, v_stack = self.stacks
        ls = 0 if prompt else l
        hm, C1, n1, m1 = _mlstm(proj, gif, wts["b_if"], wts["conv_w"], wts["conv_b"], wts["m_norm"],
                                self.cbuf, self.C0, self.n0, self.m0, l, ls, B, T, col_m,
                                BF16 if prompt else F32, BF16 if prompt else F32)
        if prompt:
            ha = _attn_prompt(proj, k_stack, v_stack, self.bias, l, B, T, col_q)
        else:
            ha = _attn_sample(proj, k_stack, v_stack, self.cache_k, self.cache_v, self.bias_c, self.bias_n,
                              l, B, T, col_q)
        self.Cs.append(C1)
        self.ns.append(n1.reshape(B, M_HEADS, M_DK))
        self.ms.append(m1.reshape(B, M_HEADS))
        pre = proj.reshape(B, T, -1)[:, :, col_m:col_m + QK_WIDTH]
        if T >= CONV_W - 1:
            self.convs.append(pre[:, T - (CONV_W - 1):])
        else:
            self.convs.append(jnp.concatenate([self.cbuf[ls][:, CONV_PAD - (CONV_W - 1) + T:], pre], axis=1))
        return _outproj(x, hm, ha, proj, wts["w_pm"], wts["w_pa"], wts["w_out"], l, _row_tile(B * T, 512))

    def finish(self, x):
        B, T, D = self.shape
        y = _final_norm(x, self.wts["ln_f"], self.tm).reshape(B, T, D)
        stk = lambda xs: jnp.stack(xs, axis=0)
        keep = min(MAX_WINDOW, T)
        k_stack, v_stack = self.stacks
        k_new = k_stack.reshape(self.depth, B, T, A_HEADS, A_DH)[:, :, T - keep:]
        v_new = v_stack.reshape(self.depth, B, T, A_HEADS, A_DH)[:, :, T - keep:]
        return y, (k_new, v_new, stk(self.Cs), stk(self.ns), stk(self.ms), stk(self.convs))


def kernel(x_prompt, x_sample, cache_k, cache_v, state_C, state_n, state_m, state_conv, w_in, conv_w, conv_b, b_if, m_norm, w_pm, w_pa, w_out, rel_table, ln_ffa, w_ffa_in, w_ffa_out, ln_mix, ln_ffb, w_ffb_in, w_ffb_out, ln_f):
    depth, D, _ = w_in.shape
    o_if = QK_WIDTH + 2 * M_WIDTH
    o_a = o_if + 2 * M_HEADS
    w_t = jnp.swapaxes(w_in, 1, 2)
    w_if32 = jnp.pad(w_t[:, o_if:o_a, :], ((0, 0), (0, LANE - 2 * M_HEADS), (0, 0)))
    w_if_hi = w_if32.astype(BF16)
    w_if_lo = (w_if32 - w_if_hi.astype(F32)).astype(BF16)
    wts = dict(
        w_ab=_wprep(w_t, o_if, o_a - o_if, W_TILE),
        w_if=jnp.stack([w_if_hi, w_if_lo], axis=1),
        b_if=jnp.pad(b_if.astype(F32), ((0, 0), (0, LANE - 2 * M_HEADS))).reshape(depth, 1, LANE),
        conv_w=conv_w.astype(F32),
        conv_b=conv_b.astype(F32).reshape(depth, 1, QK_WIDTH),
        m_norm=m_norm.astype(F32).reshape(depth, 1, M_WIDTH),
        w_pm=w_pm.astype(BF16), w_pa=w_pa.astype(BF16), w_out=w_out.astype(BF16),
        ln_ffa=ln_ffa.astype(F32).reshape(depth, 1, D), ln_mix=ln_mix.astype(F32).reshape(depth, 1, D),
        ln_ffb=ln_ffb.astype(F32).reshape(depth, 1, D), ln_f=ln_f.astype(F32).reshape(1, D),
        w_ffa_in=w_ffa_in.astype(BF16), w_ffa_out=w_ffa_out.astype(BF16),
        w_ffb_in=w_ffb_in.astype(BF16), w_ffb_out=w_ffb_out.astype(BF16),
    )
    sb = _step_biases(rel_table)
    prompt = _Path(x_prompt, wts, sb, None)
    sample = _Path(x_sample, wts, sb, (cache_k, cache_v, state_C, state_n, state_m, state_conv))
    FF = wts["w_ffa_out"].shape[1]
    tf = 512 if FF % 512 == 0 else FF
    xp, xs = prompt.x0, sample.x0
    for l in range(depth):
        xp, xs = _ffn(xp, xs, wts["ln_ffa"], wts["w_ffa_in"], wts["w_ffa_out"], l, prompt.tm, tf)
        xp = prompt.mixer(xp, l)
        xs = sample.mixer(xs, l)
        xp, xs = _ffn(xp, xs, wts["ln_ffb"], wts["w_ffb_in"], wts["w_ffb_out"], l, prompt.tm, tf)
    y_p, (k_p, v_p, C_p, n_p, m_p, conv_p) = prompt.finish(xp)
    y_s, (k_s, v_s, C_s, n_s, m_s, conv_s) = sample.finish(xs)
    return (y_p, y_s, k_p, v_p, C_p, n_p, m_p, conv_p, k_s, v_s, C_s, n_s, m_s, conv_s)
```

```python
import functools
import math

import jax
import jax.numpy as jnp
from jax import lax
from jax.experimental import pallas as pl
from jax.experimental.pallas import tpu as pltpu

F32 = jnp.float32
BF16 = jnp.bfloat16

M_HEADS = 4
M_DK = 256
M_DV = 256
M_WIDTH = M_HEADS * M_DV
QK_WIDTH = 2 * M_HEADS * M_DK
CONV_W = 4
CHUNK = 128
A_HEADS = 8
A_DH = 128
A_WIDTH = A_HEADS * A_DH
PATTERNS = ((128, 1), (512, 4), (2048, 16))
MAX_WINDOW = 2048
Q_BLOCK = 128
N_BUCKETS = 32
MAX_DISTANCE = MAX_WINDOW
EPS = 1e-6
N_STEPS = 128
LANE = 128
CONV_PAD = 8
VMEM_LIMIT = 58 * 1024 * 1024


def _dot(a, b):
    return jnp.dot(a, b, preferred_element_type=F32)


def _dot_nt(a, b):
    return lax.dot_general(a, b, (((1,), (1,)), ((), ())), preferred_element_type=F32)


def _dot_tn(a, b):
    return lax.dot_general(a, b, (((0,), (0,)), ((), ())), preferred_element_type=F32)


def _rms(x, g):
    return x * lax.rsqrt(jnp.mean(x * x, axis=-1, keepdims=True) + EPS) * g


def _params(sem):
    return pltpu.CompilerParams(dimension_semantics=sem, vmem_limit_bytes=VMEM_LIMIT)


def _ffn_kernel(x_ref, xs_ref, g_ref, wg_ref, wu_ref, wo_ref, o_ref, os_ref, h_ref):
    i = pl.program_id(0)
    j = pl.program_id(1)
    last = pl.num_programs(1) - 1
    tm = x_ref.shape[0]
    ts = xs_ref.shape[0]

    @pl.when(j == 0)
    def _():
        h_ref[0:tm, :] = _rms(x_ref[...], g_ref[...]).astype(BF16)
        o_ref[...] = jnp.zeros_like(o_ref)

    @pl.when((i == 0) & (j == 0))
    def _():
        h_ref[tm:tm + ts, :] = _rms(xs_ref[...], g_ref[...]).astype(BF16)
        os_ref[...] = jnp.zeros_like(os_ref)

    def swiglu(rows):
        h = h_ref[0:rows, :]
        gate = _dot(h, wg_ref[...])
        up = _dot(h, wu_ref[...])
        act = (gate * jax.nn.sigmoid(gate) * up).astype(BF16)
        return _dot(act, wo_ref[...])

    @pl.when(i == 0)
    def _():
        res = swiglu(tm + ts)
        o_ref[...] += res[0:tm, :]
        os_ref[...] += res[tm:tm + ts, :]

    @pl.when(i > 0)
    def _():
        o_ref[...] += swiglu(tm)

    @pl.when(j == last)
    def _():
        o_ref[...] = x_ref[...] + 0.5 * o_ref[...]

    @pl.when((i == 0) & (j == last))
    def _():
        os_ref[...] = xs_ref[...] + 0.5 * os_ref[...]


def _ffn(x, xs, g, w_in, w_out, l, tm, tf):
    M, D = x.shape
    Ms = xs.shape[0]
    FF = w_out.shape[1]
    nj = FF // tf
    return pl.pallas_call(
        _ffn_kernel,
        grid=(M // tm, nj),
        in_specs=[
            pl.BlockSpec((tm, D), lambda i, j: (i, 0)),
            pl.BlockSpec((Ms, D), lambda i, j: (0, 0)),
            pl.BlockSpec((None, 1, D), lambda i, j: (l, 0, 0)),
            pl.BlockSpec((None, D, tf), lambda i, j: (l, 0, j)),
            pl.BlockSpec((None, D, tf), lambda i, j: (l, 0, j + nj)),
            pl.BlockSpec((None, tf, D), lambda i, j: (l, j, 0)),
        ],
        out_specs=[
            pl.BlockSpec((tm, D), lambda i, j: (i, 0)),
            pl.BlockSpec((Ms, D), lambda i, j: (0, 0)),
        ],
        out_shape=[jax.ShapeDtypeStruct((M, D), F32), jax.ShapeDtypeStruct((Ms, D), F32)],
        scratch_shapes=[pltpu.VMEM((tm + Ms, D), BF16)],
        compiler_params=_params(("arbitrary", "arbitrary")),
        name="ffn",
    )(x, xs, g, w_in, w_in, w_out)


W_TILE = 1024
SUBLANE = 8


def _wprep_kernel(w1_ref, w2_ref, o_ref, *, nA, shift):
    j = pl.program_id(1)

    @pl.when(j < nA)
    def _():
        o_ref[...] = w1_ref[...].astype(BF16)

    @pl.when(j >= nA)
    def _():
        o_ref[...] = jnp.concatenate([w1_ref[shift:, :], w2_ref[...]], axis=0).astype(BF16)


def _wprep(w_t, n_a, shift, tn):
    depth, n_in, D = w_t.shape
    assert n_a % tn == 0 and (n_in - shift) % tn == 0 and shift == SUBLANE
    nj = (n_in - shift) // tn
    kern = functools.partial(_wprep_kernel, nA=n_a // tn, shift=shift)
    return pl.pallas_call(
        kern,
        grid=(depth, nj),
        in_specs=[
            pl.BlockSpec((None, tn, D), lambda l, j: (l, j, 0)),
            pl.BlockSpec((None, shift, D), lambda l, j: (l, (j + 1) * (tn // shift), 0)),
        ],
        out_specs=pl.BlockSpec((None, tn, D), lambda l, j: (l, j, 0)),
        out_shape=jax.ShapeDtypeStruct((depth, n_in - shift, D), BF16),
        compiler_params=_params(("parallel", "parallel")),
        name="wprep",
    )(w_t, w_t)


def _norm_rows(x_ref, g_ref, h_ref, rows):
    h32 = _rms(x_ref[...], g_ref[...])
    hb = h32.astype(BF16)
    h_ref[rows, :] = hb
    return h32, hb


def _inproj_kernel(x_ref, xs_ref, g_ref, w_ref, wif_ref, o_ref, os_ref, gif_ref, gifs_ref, h_ref):
    i = pl.program_id(0)
    j = pl.program_id(1)
    tm = x_ref.shape[0]
    ts = xs_ref.shape[0]

    def gates(h32, hb):
        hl = (h32 - hb.astype(F32)).astype(BF16)
        return _dot_nt(hb, wif_ref[0]) + _dot_nt(hb, wif_ref[1]) + _dot_nt(hl, wif_ref[0])

    @pl.when(j == 0)
    def _():
        gif_ref[...] = gates(*_norm_rows(x_ref, g_ref, h_ref, slice(0, tm)))

    @pl.when((i == 0) & (j == 0))
    def _():
        gifs_ref[...] = gates(*_norm_rows(xs_ref, g_ref, h_ref, slice(tm, tm + ts)))

    @pl.when(i == 0)
    def _():
        res = _dot_nt(h_ref[...], w_ref[...])
        o_ref[...] = res[0:tm, :]
        os_ref[...] = res[tm:tm + ts, :]

    @pl.when(i > 0)
    def _():
        o_ref[...] = _dot_nt(h_ref[0:tm, :], w_ref[...])


def _inproj(x, xs, g, w_ab, w_if, l, tm, tn):
    M, D = x.shape
    Ms = xs.shape[0]
    nG, nA, nQ = 2 * D // tn, (QK_WIDTH + 2 * M_WIDTH) // tn, A_WIDTH // tn
    NP = 2 * D + QK_WIDTH + 2 * M_WIDTH + A_WIDTH
    nj = NP // tn
    w_idx = lambda j: jnp.where(j < nG, j + nA + 3 * nQ, j - nG)
    return pl.pallas_call(
        _inproj_kernel,
        grid=(M // tm, nj),
        in_specs=[
            pl.BlockSpec((tm, D), lambda i, j: (i, 0)),
            pl.BlockSpec((Ms, D), lambda i, j: (0, 0)),
            pl.BlockSpec((None, 1, D), lambda i, j: (l, 0, 0)),
            pl.BlockSpec((None, tn, D), lambda i, j: (l, w_idx(j), 0)),
            pl.BlockSpec((None, 2, LANE, D), lambda i, j: (l, 0, 0, 0)),
        ],
        out_specs=[
            pl.BlockSpec((tm, tn), lambda i, j: (i, j)),
            pl.BlockSpec((Ms, tn), lambda i, j: (0, jnp.where(i == 0, j, nj - 1))),
            pl.BlockSpec((tm, LANE), lambda i, j: (i, 0)),
            pl.BlockSpec((Ms, LANE), lambda i, j: (0, 0)),
        ],
        out_shape=[jax.ShapeDtypeStruct((M, NP), F32), jax.ShapeDtypeStruct((Ms, NP), F32),
                   jax.ShapeDtypeStruct((M, LANE), F32), jax.ShapeDtypeStruct((Ms, LANE), F32)],
        scratch_shapes=[pltpu.VMEM((tm + Ms, D), BF16)],
        compiler_params=_params(("arbitrary", "arbitrary")),
        name="inproj",
    )(x, xs, g, w_ab, w_if)


def _kvproj_kernel(x_ref, g_ref, w_ref, *refs, nQ):
    k_ref, v_ref, h_ref = refs[-3:]
    j = pl.program_id(1)

    @pl.when(j == 0)
    def _():
        h_ref[...] = _rms(x_ref[...], g_ref[...]).astype(BF16)

    @pl.when(j < nQ)
    def _():
        k_ref[...] = _dot_nt(h_ref[...], w_ref[...])

    @pl.when(j >= nQ)
    def _():
        v_ref[...] = _dot_nt(h_ref[...], w_ref[...])


def _kvproj(x, g, w_ab, stacks, l, tm, tn):
    M, D = x.shape
    depth = w_ab.shape[0]
    nA, nQ = (QK_WIDTH + 2 * M_WIDTH) // tn, A_WIDTH // tn
    kern = functools.partial(_kvproj_kernel, nQ=nQ)
    carried = () if stacks is None else tuple(stacks)
    return pl.pallas_call(
        kern,
        grid=(M // tm, 2 * nQ),
        in_specs=[
            pl.BlockSpec((tm, D), lambda i, j: (i, 0)),
            pl.BlockSpec((None, 1, D), lambda i, j: (l, 0, 0)),
            pl.BlockSpec((None, tn, D), lambda i, j: (l, nA + nQ + j, 0)),
        ] + [pl.BlockSpec(memory_space=pl.ANY) for _ in carried],
        out_specs=[
            pl.BlockSpec((None, tm, tn), lambda i, j: (l, i, jnp.minimum(j, nQ - 1))),
            pl.BlockSpec((None, tm, tn), lambda i, j: (l, i, jnp.maximum(j - nQ, 0))),
        ],
        out_shape=[jax.ShapeDtypeStruct((depth, M, A_WIDTH), F32)] * 2,
        scratch_shapes=[pltpu.VMEM((tm, D), BF16)],
        input_output_aliases={3 + n: n for n in range(len(carried))},
        compiler_params=_params(("parallel", "arbitrary")),
        name="kvproj",
    )(x, g, w_ab, *carried)


def _mlstm_kernel(qp_ref, kp_ref, v_ref, og_ref, gif_ref, bif_ref, cw_ref, cb_ref, mn_ref,
                  cbuf_ref, C0_ref, n0_ref, m0_ref,
                  hm_ref, C_ref, n_ref, m_ref, xq_ref, xk_ref, *, L, mm_dtype):
    c = pl.program_id(1)
    half = QK_WIDTH // 2

    @pl.when(c == 0)
    def _():
        xq_ref[0:CONV_PAD, :] = cbuf_ref[:, 0:half]
        xk_ref[0:CONV_PAD, :] = cbuf_ref[:, half:QK_WIDTH]
        C_ref[...] = C0_ref[...]
        n_ref[...] = n0_ref[...]
        m_ref[...] = m0_ref[...]

    @pl.when(c > 0)
    def _():
        xq_ref[0:CONV_PAD, :] = xq_ref[L:L + CONV_PAD, :]
        xk_ref[0:CONV_PAD, :] = xk_ref[L:L + CONV_PAD, :]

    xq_ref[CONV_PAD:CONV_PAD + L, :] = qp_ref[...]
    xk_ref[CONV_PAD:CONV_PAD + L, :] = kp_ref[...]

    cw = cw_ref[...]
    cb = cb_ref[...]

    def conv_silu(x_ref, w, b):
        xs = x_ref[...]
        y = b + xs * w[CONV_W - 1:CONV_W, :]
        for s in range(1, CONV_W):
            y = y + pltpu.roll(xs, s, axis=0) * w[CONV_W - 1 - s:CONV_W - s, :]
        y = y[CONV_PAD:, :]
        return y * jax.nn.sigmoid(y)

    qa = conv_silu(xq_ref, cw[:, 0:half], cb[:, 0:half])
    ka = conv_silu(xk_ref, cw[:, half:QK_WIDTH], cb[:, half:QK_WIDTH]) * (M_DK ** -0.5)
    g = gif_ref[...] + bif_ref[...]

    ti = lax.broadcasted_iota(jnp.int32, (L, L), 0)
    si = lax.broadcasted_iota(jnp.int32, (L, L), 1)
    eye = ti == si
    low = si <= ti
    upp = ti <= si

    for h in range(M_HEADS):
        sl = slice(h * M_DV, (h + 1) * M_DV)
        ig_col = g[:, h:h + 1]
        gf = g[:, M_HEADS + h:M_HEADS + h + 1]
        lf_col = jnp.minimum(gf, 0.0) - jnp.log1p(jnp.exp(-jnp.abs(gf)))
        ig_row = jnp.sum(jnp.where(eye, ig_col, 0.0), axis=0, keepdims=True)
        lf_row = jnp.sum(jnp.where(eye, lf_col, 0.0), axis=0, keepdims=True)
        F_col = jnp.sum(jnp.where(low, lf_row, 0.0), axis=1, keepdims=True)
        F_row = jnp.sum(jnp.where(upp, lf_col, 0.0), axis=0, keepdims=True)
        m0 = m_ref[h]
        C0 = C_ref[h]
        n0 = n_ref[h]
        logD = jnp.where(low, F_col - F_row + ig_row, -jnp.inf)
        lst = F_col + m0
        m_col = jnp.maximum(lst, jnp.max(logD, axis=1, keepdims=True))
        q = qa[:, sl]
        k = ka[:, sl]
        qb = q.astype(mm_dtype)
        kb = k.astype(mm_dtype)
        vb = v_ref[:, sl].astype(mm_dtype)
        wgt = jnp.exp(logD - m_col) * _dot_nt(qb, kb)
        sc = jnp.exp(lst - m_col)
        num = _dot(wgt.astype(mm_dtype), vb) + sc * _dot(qb, C0.astype(mm_dtype))
        den = jnp.sum(wgt, axis=1, keepdims=True) + sc * jnp.sum(q * n0, axis=1, keepdims=True)
        hh = num / jnp.maximum(jnp.abs(den), jnp.exp(-m_col))
        FT = F_col[L - 1:L, :]
        lT = FT + m0
        m_new = jnp.maximum(lT, jnp.max(FT - F_row + ig_row, axis=1, keepdims=True))
        e_col = jnp.exp(FT - F_col + ig_col - m_new)
        sT = jnp.exp(lT - m_new)
        ke = k * e_col
        C_ref[h] = sT * C0 + _dot_tn(ke.astype(mm_dtype), vb)
        n_ref[h] = sT * n0 + jnp.sum(ke, axis=0, keepdims=True)
        m_ref[h] = m_new
        hn = hh * lax.rsqrt(jnp.mean(hh * hh, axis=1, keepdims=True) + EPS)
        out = jax.nn.sigmoid(og_ref[:, sl]) * hn * mn_ref[:, sl]
        hm_ref[:, sl] = out.astype(hm_ref.dtype)


def _mlstm(proj, gif, b_if, conv_w, conv_b, m_norm, cbuf, C0, n0, m0, l, ls, B, T, col0, out_dtype, mm_dtype):
    L = CHUNK if T % CHUNK == 0 else T
    nc = T // L
    qo = col0 // M_WIDTH
    kern = functools.partial(_mlstm_kernel, L=L, mm_dtype=mm_dtype)
    row = lambda b, c: b * nc + c
    return pl.pallas_call(
        kern,
        grid=(B, nc),
        in_specs=[
            pl.BlockSpec((L, M_WIDTH), lambda b, c: (row(b, c), qo)),
            pl.BlockSpec((L, M_WIDTH), lambda b, c: (row(b, c), qo + 1)),
            pl.BlockSpec((L, M_WIDTH), lambda b, c: (row(b, c), qo + 2)),
            pl.BlockSpec((L, M_WIDTH), lambda b, c: (row(b, c), qo + 3)),
            pl.BlockSpec((L, LANE), lambda b, c: (row(b, c), 0)),
            pl.BlockSpec((None, 1, LANE), lambda b, c: (l, 0, 0)),
            pl.BlockSpec((None, CONV_W, QK_WIDTH), lambda b, c: (l, 0, 0)),
            pl.BlockSpec((None, 1, QK_WIDTH), lambda b, c: (l, 0, 0)),
            pl.BlockSpec((None, 1, M_WIDTH), lambda b, c: (l, 0, 0)),
            pl.BlockSpec((None, None, CONV_PAD, QK_WIDTH), lambda b, c: (ls, b, 0, 0)),
            pl.BlockSpec((None, None, M_HEADS, M_DK, M_DV), lambda b, c: (ls, b, 0, 0, 0)),
            pl.BlockSpec((None, None, M_HEADS, 1, M_DK), lambda b, c: (ls, b, 0, 0, 0)),
            pl.BlockSpec((None, None, M_HEADS, 1, 1), lambda b, c: (ls, b, 0, 0, 0)),
        ],
        out_specs=[
            pl.BlockSpec((L, M_WIDTH), lambda b, c: (row(b, c), 0)),
            pl.BlockSpec((None, M_HEADS, M_DK, M_DV), lambda b, c: (b, 0, 0, 0)),
            pl.BlockSpec((None, M_HEADS, 1, M_DK), lambda b, c: (b, 0, 0, 0)),
            pl.BlockSpec((None, M_HEADS, 1, 1), lambda b, c: (b, 0, 0, 0)),
        ],
        out_shape=[
            jax.ShapeDtypeStruct((B * T, M_WIDTH), out_dtype),
            jax.ShapeDtypeStruct((B, M_HEADS, M_DK, M_DV), F32),
            jax.ShapeDtypeStruct((B, M_HEADS, 1, M_DK), F32),
            jax.ShapeDtypeStruct((B, M_HEADS, 1, 1), F32),
        ],
        scratch_shapes=[pltpu.VMEM((L + CONV_PAD, M_WIDTH), F32), pltpu.VMEM((L + CONV_PAD, M_WIDTH), F32)],
        compiler_params=_params(("parallel", "arbitrary")),
        name="mlstm",
    )(proj, proj, proj, proj, gif, b_if, conv_w, conv_b, m_norm, cbuf, C0, n0, m0)


STRIDE = 4


def _attn_kernel(q_ref, k_ref, v_ref, bias_ref, o_ref,
                 p4, p16, qbs, kbs, vbs, ops, lps, o4, l4, on, ln, *, S):
    scale = A_DH ** -0.5
    QB = Q_BLOCK
    nunits = S // QB
    G = S // STRIDE
    srcs = (q_ref, k_ref, v_ref)

    def load_operands(p, get):
        qb, kb, vb = qbs.at[p], kbs.at[p], vbs.at[p]
        kb[0:QB, :] = jnp.zeros((QB, A_DH), BF16)
        vb[0:QB, 0:A_DH] = jnp.zeros((QB, A_DH), BF16)
        vb[:, A_DH:2 * A_DH] = jnp.ones((S + QB, A_DH), BF16)
        qb[...] = (get(0) * scale).astype(BF16)
        kb[QB:QB + S, :] = get(1).astype(BF16)
        vb[QB:QB + S, 0:A_DH] = get(2).astype(BF16)

    def unit(u, p, nblk, dst_o, dst_l):
        qb, kb, vb = qbs.at[p], kbs.at[p], vbs.at[p]
        r0 = u * QB
        qv = qb[r0:r0 + QB, :]
        if nblk == 1:
            kw = kb[QB + r0:2 * QB + r0, :]
            vw = vb[QB + r0:2 * QB + r0, :]
            bias = bias_ref[p, 0, :, QB:2 * QB]
        else:
            kw = kb[r0:r0 + 2 * QB, :]
            vw = vb[r0:r0 + 2 * QB, :]
            bias = bias_ref[p, 1 if u % nblk == 0 else 0]
        s = _dot_nt(qv, kw) + bias
        m = jnp.max(s, axis=1, keepdims=True)
        pv = _dot(jnp.exp(s - m).astype(BF16), vw)
        den = pv[:, A_DH:2 * A_DH]
        dst_o[r0:r0 + QB, :] = pv[:, 0:A_DH] / den
        dst_l[r0:r0 + QB, :] = m + jnp.log(den)

    load_operands(0, lambda a: srcs[a][...])
    for u in range(nunits):
        unit(u, 0, nunits, on.at[0], ln.at[0])

    for a in range(3):
        for r in range(STRIDE):
            p4[a, r * G:(r + 1) * G, :] = srcs[a][pl.ds(r, G, stride=STRIDE), :]
    load_operands(1, lambda a: p4[a])
    for u in range(nunits):
        unit(u, 1, G // QB, ops.at[0], lps.at[0])
    for r in range(STRIDE):
        on[1, pl.ds(r, G, stride=STRIDE), :] = ops[0, r * G:(r + 1) * G, :]
        ln[1, pl.ds(r, G, stride=STRIDE), :] = lps[0, r * G:(r + 1) * G, :]

    for a in range(3):
        for r in range(STRIDE):
            for c in range(STRIDE):
                u = r * STRIDE + c
                p16[a, u * QB:(u + 1) * QB, :] = p4[a, pl.ds(r * G + c, QB, stride=STRIDE), :]
    load_operands(2, lambda a: p16[a])
    for u in range(nunits):
        unit(u, 2, 1, ops.at[1], lps.at[1])
    for r in range(STRIDE):
        for c in range(STRIDE):
            u = r * STRIDE + c
            o4[pl.ds(r * G + c, QB, stride=STRIDE), :] = ops[1, u * QB:(u + 1) * QB, :]
            l4[pl.ds(r * G + c, QB, stride=STRIDE), :] = lps[1, u * QB:(u + 1) * QB, :]
    for r in range(STRIDE):
        on[2, pl.ds(r, G, stride=STRIDE), :] = o4[r * G:(r + 1) * G, :]
        ln[2, pl.ds(r, G, stride=STRIDE), :] = l4[r * G:(r + 1) * G, :]

    def combine(bi, carry):
        rows = pl.ds(pl.multiple_of(bi * QB, QB), QB)
        ls = [ln[p, rows, :] for p in range(len(PATTERNS))]
        lmax = functools.reduce(jnp.maximum, ls)
        tot = jnp.zeros((QB, A_DH), F32)
        wsum = jnp.zeros((QB, A_DH), F32)
        for p in range(len(PATTERNS)):
            a = jnp.exp(ls[p] - lmax)
            tot = tot + a * on[p, rows, :]
            wsum = wsum + a
        o_ref[rows, :] = (tot / wsum).astype(o_ref.dtype)
        return carry

    lax.fori_loop(0, nunits, combine, 0)


def _attn_prompt(proj, k_stack, v_stack, bias, l, B, S, qcol):
    assert S == N_STEPS * PATTERNS[-1][1] and PATTERNS[1][1] == STRIDE and PATTERNS[2][1] == STRIDE * STRIDE
    H = A_HEADS
    cb = qcol // A_DH
    kern = functools.partial(_attn_kernel, S=S)
    npat = len(PATTERNS)
    return pl.pallas_call(
        kern,
        grid=(B, H),
        in_specs=[
            pl.BlockSpec((S, A_DH), lambda b, h: (b, cb + h)),
            pl.BlockSpec((None, S, A_DH), lambda b, h: (l, b, h)),
            pl.BlockSpec((None, S, A_DH), lambda b, h: (l, b, h)),
            pl.BlockSpec((None, npat, 2, Q_BLOCK, 2 * Q_BLOCK), lambda b, h: (h, 0, 0, 0, 0)),
        ],
        out_specs=pl.BlockSpec((S, A_DH), lambda b, h: (b, h)),
        out_shape=jax.ShapeDtypeStruct((B * S, A_WIDTH), BF16),
        scratch_shapes=[
            pltpu.VMEM((3, S, A_DH), F32),
            pltpu.VMEM((3, S, A_DH), F32),
            pltpu.VMEM((npat, S, A_DH), BF16),
            pltpu.VMEM((npat, S + Q_BLOCK, A_DH), BF16),
            pltpu.VMEM((npat, S + Q_BLOCK, 2 * A_DH), BF16),
            pltpu.VMEM((2, S, A_DH), F32),
            pltpu.VMEM((2, S, A_DH), F32),
            pltpu.VMEM((S, A_DH), F32),
            pltpu.VMEM((S, A_DH), F32),
            pltpu.VMEM((npat, S, A_DH), F32),
            pltpu.VMEM((npat, S, A_DH), F32),
        ],
        compiler_params=_params(("parallel", "parallel")),
        name="attn_prompt",
    )(proj, k_stack, v_stack, bias)


def _attn_s_kernel(q_ref, kn_ref, vn_ref, kc_ref, vc_ref, bc_ref, bn_ref, o_ref, *, Lbuf):
    H = A_HEADS
    npat = len(PATTERNS)
    for h in range(H):
        hs = slice(h * A_DH, (h + 1) * A_DH)
        q = q_ref[:, hs] * (A_DH ** -0.5)
        kc = kc_ref[pl.ds(h, Lbuf, stride=H), :]
        vc = vc_ref[pl.ds(h, Lbuf, stride=H), :]
        s_c = _dot_nt(q, kc)
        s_n = _dot_nt(q, kn_ref[:, hs])
        mx = None
        for p in range(npat):
            mp = jnp.maximum(jnp.max(s_c + bc_ref[h, p], axis=1, keepdims=True),
                             jnp.max(s_n + bn_ref[h, p], axis=1, keepdims=True))
            mx = mp if mx is None else jnp.maximum(mx, mp)
        e_c = jnp.exp(s_c + bc_ref[h, 0] - mx)
        e_n = jnp.exp(s_n + bn_ref[h, 0] - mx)
        for p in range(1, npat):
            e_c = e_c + jnp.exp(s_c + bc_ref[h, p] - mx)
            e_n = e_n + jnp.exp(s_n + bn_ref[h, p] - mx)
        num = _dot(e_c, vc) + _dot(e_n, vn_ref[:, hs])
        den = jnp.sum(e_c, axis=1, keepdims=True) + jnp.sum(e_n, axis=1, keepdims=True)
        o_ref[:, hs] = num / den


def _attn_sample(proj, k_stack, v_stack, cache_k, cache_v, bias_c, bias_n, l, B, T, qcol):
    H = A_HEADS
    Lbuf = cache_k.shape[2] // H
    npat = len(PATTERNS)
    kern = functools.partial(_attn_s_kernel, Lbuf=Lbuf)
    return pl.pallas_call(
        kern,
        grid=(B,),
        in_specs=[
            pl.BlockSpec((T, A_WIDTH), lambda b: (b, qcol // A_WIDTH)),
            pl.BlockSpec((None, T, A_WIDTH), lambda b: (l, b, 0)),
            pl.BlockSpec((None, T, A_WIDTH), lambda b: (l, b, 0)),
            pl.BlockSpec((None, None, Lbuf * H, A_DH), lambda b: (l, b, 0, 0)),
            pl.BlockSpec((None, None, Lbuf * H, A_DH), lambda b: (l, b, 0, 0)),
            pl.BlockSpec((H, npat, T, Lbuf), lambda b: (0, 0, 0, 0)),
            pl.BlockSpec((H, npat, T, T), lambda b: (0, 0, 0, 0)),
        ],
        out_specs=pl.BlockSpec((T, A_WIDTH), lambda b: (b, 0)),
        out_shape=jax.ShapeDtypeStruct((B * T, A_WIDTH), F32),
        compiler_params=_params(("parallel",)),
        name="attn_sample",
    )(proj, k_stack, v_stack, cache_k, cache_v, bias_c, bias_n)


def _outproj_kernel(x_ref, hm_ref, ha_ref, gm_ref, ga_ref, wpm_ref, wpa_ref, wo_ref, o_ref):
    pm = _dot(hm_ref[...].astype(BF16), wpm_ref[...])
    pa = _dot(ha_ref[...].astype(BF16), wpa_ref[...])
    merged = jax.nn.sigmoid(gm_ref[...]) * pm + jax.nn.sigmoid(ga_ref[...]) * pa
    o_ref[...] = x_ref[...] + _dot(merged.astype(BF16), wo_ref[...])


def _outproj(x, hm, ha, proj, w_pm, w_pa, w_out, l, tm):
    M, D = x.shape
    once = pl.Buffered(1)
    return pl.pallas_call(
        _outproj_kernel,
        grid=(M // tm,),
        in_specs=[
            pl.BlockSpec((tm, D), lambda i: (i, 0)),
            pl.BlockSpec((tm, M_WIDTH), lambda i: (i, 0)),
            pl.BlockSpec((tm, A_WIDTH), lambda i: (i, 0)),
            pl.BlockSpec((tm, D), lambda i: (i, 0)),
            pl.BlockSpec((tm, D), lambda i: (i, 1)),
            pl.BlockSpec((None, M_WIDTH, D), lambda i: (l, 0, 0), pipeline_mode=once),
            pl.BlockSpec((None, A_WIDTH, D), lambda i: (l, 0, 0), pipeline_mode=once),
            pl.BlockSpec((None, D, D), lambda i: (l, 0, 0), pipeline_mode=once),
        ],
        out_specs=pl.BlockSpec((tm, D), lambda i: (i, 0)),
        out_shape=jax.ShapeDtypeStruct((M, D), F32),
        compiler_params=_params(("parallel",)),
        name="outproj",
    )(x, hm, ha, proj, proj, w_pm, w_pa, w_out)


def _norm_kernel(x_ref, g_ref, o_ref):
    o_ref[...] = _rms(x_ref[...], g_ref[...])


def _final_norm(x, g, tm):
    M, D = x.shape
    return pl.pallas_call(
        _norm_kernel,
        grid=(M // tm,),
        in_specs=[pl.BlockSpec((tm, D), lambda i: (i, 0)), pl.BlockSpec((1, D), lambda i: (0, 0))],
        out_specs=pl.BlockSpec((tm, D), lambda i: (i, 0)),
        out_shape=jax.ShapeDtypeStruct((M, D), F32),
        compiler_params=_params(("parallel",)),
        name="final_norm",
    )(x, g)


def _t5_bucket(dist):
    exact = N_BUCKETS // 2
    d32 = jnp.maximum(dist, 1).astype(F32)
    large = exact + (jnp.log(d32 / exact) / math.log(MAX_DISTANCE / exact) * (N_BUCKETS - exact)).astype(jnp.int32)
    large = jnp.minimum(large, N_BUCKETS - 1)
    return jnp.where(dist < exact, dist, large)


def _step_biases(rel_table):
    assert all(w // d == N_STEPS for w, d in PATTERNS)
    buckets = jnp.stack([_t5_bucket(d * jnp.arange(N_STEPS + 1, dtype=jnp.int32)) for (w, d) in PATTERNS])
    onehot = (buckets[:, :, None] == jnp.arange(N_BUCKETS)[None, None, :]).astype(F32)
    return jnp.einsum("pjn,nh->phj", onehot, rel_table.astype(F32), precision=lax.Precision.HIGHEST)


def _toeplitz(g, rows, cols, off):
    n = g.shape[-1]
    width = off + cols
    assert width <= n - 1 and off - (rows - 1) >= 0
    flat = jnp.tile(g, (1,) * (g.ndim - 1) + (rows,))[..., :rows * (n - 1)]
    return flat.reshape(g.shape[:-1] + (rows, n - 1))[..., off:off + cols]


def _prompt_bias(sb):
    Q = Q_BLOCK
    ninf = lambda n: jnp.full(sb.shape[:-1] + (n,), -jnp.inf, F32)
    g = jnp.concatenate([ninf(Q - 1), sb[..., ::-1], ninf(Q)], axis=-1)
    base = _toeplitz(g, Q, 2 * Q, Q - 1)
    first = jnp.where(jnp.arange(2 * Q) >= Q, base, -jnp.inf)
    return jnp.transpose(jnp.stack([base, first], axis=2), (1, 0, 2, 3, 4))


def _sample_bias(sb, T, Lbuf):
    n = Lbuf + T
    tabs = []
    for p, (w, d) in enumerate(PATTERNS):
        bd = jnp.concatenate([sb[p][..., None], jnp.full(sb[p].shape + (d - 1,), -jnp.inf, F32)], axis=-1)
        bd = bd.reshape(sb.shape[1], -1)
        bd = jnp.concatenate([bd, jnp.full((sb.shape[1], n), -jnp.inf, F32)], axis=-1)[:, :n]
        g = jnp.concatenate([bd[:, ::-1], jnp.full((sb.shape[1], T), -jnp.inf, F32)], axis=-1)
        tabs.append(_toeplitz(g, T, n, T - 1))
    tab = jnp.stack(tabs, axis=1)
    return tab[..., :Lbuf], tab[..., Lbuf:]


def _row_tile(M, cap):
    return cap if M % cap == 0 else M


class _Path:
    def __init__(self, x3, wts, sb, states):
        self.wts = wts
        self.shape = x3.shape
        self.prompt = states is None
        self.depth = wts["w_ab"].shape[0]
        self._setup(x3, sb, states)

    def _setup(self, x3, sb, states):
        B, T, D = self.shape
        wts, depth, prompt = self.wts, self.depth, self.prompt
        M = B * T
        self.tm = _row_tile(M, 1024)
        if prompt:
            self.bias = _prompt_bias(sb)
            self.cbuf = jnp.zeros((1, B, CONV_PAD, QK_WIDTH), F32)
            self.C0 = jnp.zeros((1, B, M_HEADS, M_DK, M_DV), F32)
            self.n0 = jnp.zeros((1, B, M_HEADS, 1, M_DK), F32)
            self.m0 = jnp.zeros((1, B, M_HEADS, 1, 1), F32)
        else:
            cache_k, cache_v, sC, sn, sm, sconv = states
            Lbuf = cache_k.shape[2]
            self.bias_c, self.bias_n = _sample_bias(sb, T, Lbuf)
            self.cache_k = cache_k.astype(F32).reshape(depth, B, Lbuf * A_HEADS, A_DH)
            self.cache_v = cache_v.astype(F32).reshape(depth, B, Lbuf * A_HEADS, A_DH)
            self.cbuf = jnp.pad(sconv.astype(F32), ((0, 0), (0, 0), (CONV_PAD - (CONV_W - 1), 0), (0, 0)))
            self.C0 = sC.astype(F32)
            self.n0 = sn.astype(F32).reshape(depth, B, M_HEADS, 1, M_DK)
            self.m0 = sm.astype(F32).reshape(depth, B, M_HEADS, 1, 1)
        self.x0 = x3.reshape(M, D)
        self.stacks = None
        self.Cs, self.ns, self.ms, self.convs = [], [], [], []

    def mixer(self, x, proj, gif, l):
        B, T, D = self.shape
        wts, prompt, tm = self.wts, self.prompt, self.tm
        col_m = 2 * D
        col_q = 2 * D + QK_WIDTH + 2 * M_WIDTH
        self.stacks = _kvproj(x, wts["ln_mix"], wts["w_ab"], self.stacks, l, tm, W_TILE)
        k_stack, v_stack = self.stacks
        ls = 0 if prompt else l
        hm, C1, n1, m1 = _mlstm(proj, gif, wts["b_if"], wts["conv_w"], wts["conv_b"], wts["m_norm"],
                                self.cbuf, self.C0, self.n0, self.m0, l, ls, B, T, col_m,
                                BF16 if prompt else F32, BF16 if prompt else F32)
        if prompt:
            ha = _attn_prompt(proj, k_stack, v_stack, self.bias, l, B, T, col_q)
        else:
            ha = _attn_sample(proj, k_stack, v_stack, self.cache_k, self.cache_v, self.bias_c, self.bias_n,
                              l, B, T, col_q)
        self.Cs.append(C1)
        self.ns.append(n1.reshape(B, M_HEADS, M_DK))
        self.ms.append(m1.reshape(B, M_HEADS))
        pre = proj.reshape(B, T, -1)[:, :, col_m:col_m + QK_WIDTH]
        if T >= CONV_W - 1:
            self.convs.append(pre[:, T - (CONV_W - 1):])
        else:
            self.convs.append(jnp.concatenate([self.cbuf[ls][:, CONV_PAD - (CONV_W - 1) + T:], pre], axis=1))
        return _outproj(x, hm, ha, proj, wts["w_pm"], wts["w_pa"], wts["w_out"], l, _row_tile(B * T, 512))

    def finish(self, x):
        B, T, D = self.shape
        y = _final_norm(x, self.wts["ln_f"], self.tm).reshape(B, T, D)
        stk = lambda xs: jnp.stack(xs, axis=0)
        keep = min(MAX_WINDOW, T)
        k_stack, v_stack = self.stacks
        k_new = k_stack.reshape(self.depth, B, T, A_HEADS, A_DH)[:, :, T - keep:]
        v_new = v_stack.reshape(self.depth, B, T, A_HEADS, A_DH)[:, :, T - keep:]
        return y, (k_new, v_new, stk(self.Cs), stk(self.ns), stk(self.ms), stk(self.convs))


def kernel(x_prompt, x_sample, cache_k, cache_v, state_C, state_n, state_m, state_conv, w_in, conv_w, conv_b, b_if, m_norm, w_pm, w_pa, w_out, rel_table, ln_ffa, w_ffa_in, w_ffa_out, ln_mix, ln_ffb, w_ffb_in, w_ffb_out, ln_f):
    depth, D, _ = w_in.shape
    o_if = QK_WIDTH + 2 * M_WIDTH
    o_a = o_if + 2 * M_HEADS
    w_t = jnp.swapaxes(w_in, 1, 2)
    w_if32 = jnp.pad(w_t[:, o_if:o_a, :], ((0, 0), (0, LANE - 2 * M_HEADS), (0, 0)))
    w_if_hi = w_if32.astype(BF16)
    w_if_lo = (w_if32 - w_if_hi.astype(F32)).astype(BF16)
    wts = dict(
        w_ab=_wprep(w_t, o_if, o_a - o_if, W_TILE),
        w_if=jnp.stack([w_if_hi, w_if_lo], axis=1),
        b_if=jnp.pad(b_if.astype(F32), ((0, 0), (0, LANE - 2 * M_HEADS))).reshape(depth, 1, LANE),
        conv_w=conv_w.astype(F32),
        conv_b=conv_b.astype(F32).reshape(depth, 1, QK_WIDTH),
        m_norm=m_norm.astype(F32).reshape(depth, 1, M_WIDTH),
        w_pm=w_pm.astype(BF16), w_pa=w_pa.astype(BF16), w_out=w_out.astype(BF16),
        ln_ffa=ln_ffa.astype(F32).reshape(depth, 1, D), ln_mix=ln_mix.astype(F32).reshape(depth, 1, D),
        ln_ffb=ln_ffb.astype(F32).reshape(depth, 1, D), ln_f=ln_f.astype(F32).reshape(1, D),
        w_ffa_in=w_ffa_in.astype(BF16), w_ffa_out=w_ffa_out.astype(BF16),
        w_ffb_in=w_ffb_in.astype(BF16), w_ffb_out=w_ffb_out.astype(BF16),
    )
    sb = _step_biases(rel_table)
    prompt = _Path(x_prompt, wts, sb, None)
    sample = _Path(x_sample, wts, sb, (cache_k, cache_v, state_C, state_n, state_m, state_conv))
    FF = wts["w_ffa_out"].shape[1]
    tf = 512 if FF % 512 == 0 else FF
    xp, xs = prompt.x0, sample.x0
    for l in range(depth):
        xp, xs = _ffn(xp, xs, wts["ln_ffa"], wts["w_ffa_in"], wts["w_ffa_out"], l, prompt.tm, tf)
        proj_p, proj_s, gif_p, gif_s = _inproj(xp, xs, wts["ln_mix"], wts["w_ab"], wts["w_if"], l, prompt.tm, W_TILE)
        xp = prompt.mixer(xp, proj_p, gif_p, l)
        xs = sample.mixer(xs, proj_s, gif_s, l)
        xp, xs = _ffn(xp, xs, wts["ln_ffb"], wts["w_ffb_in"], wts["w_ffb_out"], l, prompt.tm, tf)
    y_p, (k_p, v_p, C_p, n_p, m_p, conv_p) = prompt.finish(xp)
    y_s, (k_s, v_s, C_s, n_s, m_s, conv_s) = sample.finish(xs)
    return (y_p, y_s, k_p, v_p, C_p, n_p, m_p, conv_p, k_s, v_s, C_s, n_s, m_s, conv_s)
```

```python
import functools
import math

import jax
import jax.numpy as jnp
from jax import lax
from jax.experimental import pallas as pl
from jax.experimental.pallas import tpu as pltpu

F32 = jnp.float32
BF16 = jnp.bfloat16

M_HEADS = 4
M_DK = 256
M_DV = 256
M_WIDTH = M_HEADS * M_DV
QK_WIDTH = 2 * M_HEADS * M_DK
CONV_W = 4
CHUNK = 128
A_HEADS = 8
A_DH = 128
A_WIDTH = A_HEADS * A_DH
PATTERNS = ((128, 1), (512, 4), (2048, 16))
MAX_WINDOW = 2048
Q_BLOCK = 128
N_BUCKETS = 32
MAX_DISTANCE = MAX_WINDOW
EPS = 1e-6
N_STEPS = 128
LANE = 128
CONV_PAD = 8
VMEM_LIMIT = 58 * 1024 * 1024


def _dot(a, b):
    return jnp.dot(a, b, preferred_element_type=F32)


def _dot_nt(a, b):
    return lax.dot_general(a, b, (((1,), (1,)), ((), ())), preferred_element_type=F32)


def _dot_tn(a, b):
    return lax.dot_general(a, b, (((0,), (0,)), ((), ())), preferred_element_type=F32)


def _rms(x, g):
    return x * lax.rsqrt(jnp.mean(x * x, axis=-1, keepdims=True) + EPS) * g


def _params(sem):
    return pltpu.CompilerParams(dimension_semantics=sem, vmem_limit_bytes=VMEM_LIMIT)


def _ffn_kernel(x_ref, xs_ref, g_ref, wg_ref, wu_ref, wo_ref, o_ref, os_ref, h_ref):
    i = pl.program_id(0)
    j = pl.program_id(1)
    last = pl.num_programs(1) - 1
    tm = x_ref.shape[0]
    ts = xs_ref.shape[0]

    @pl.when(j == 0)
    def _():
        h_ref[0:tm, :] = _rms(x_ref[...], g_ref[...]).astype(BF16)
        o_ref[...] = jnp.zeros_like(o_ref)

    @pl.when((i == 0) & (j == 0))
    def _():
        h_ref[tm:tm + ts, :] = _rms(xs_ref[...], g_ref[...]).astype(BF16)
        os_ref[...] = jnp.zeros_like(os_ref)

    def swiglu(rows):
        h = h_ref[0:rows, :]
        gate = _dot(h, wg_ref[...])
        up = _dot(h, wu_ref[...])
        act = (gate * jax.nn.sigmoid(gate) * up).astype(BF16)
        return _dot(act, wo_ref[...])

    @pl.when(i == 0)
    def _():
        res = swiglu(tm + ts)
        o_ref[...] += res[0:tm, :]
        os_ref[...] += res[tm:tm + ts, :]

    @pl.when(i > 0)
    def _():
        o_ref[...] += swiglu(tm)

    @pl.when(j == last)
    def _():
        o_ref[...] = x_ref[...] + 0.5 * o_ref[...]

    @pl.when((i == 0) & (j == last))
    def _():
        os_ref[...] = xs_ref[...] + 0.5 * os_ref[...]


def _ffn(x, xs, g, w_in, w_out, l, tm, tf):
    M, D = x.shape
    Ms = xs.shape[0]
    FF = w_out.shape[1]
    nj = FF // tf
    return pl.pallas_call(
        _ffn_kernel,
        grid=(M // tm, nj),
        in_specs=[
            pl.BlockSpec((tm, D), lambda i, j: (i, 0)),
            pl.BlockSpec((Ms, D), lambda i, j: (0, 0)),
            pl.BlockSpec((None, 1, D), lambda i, j: (l, 0, 0)),
            pl.BlockSpec((None, D, tf), lambda i, j: (l, 0, j)),
            pl.BlockSpec((None, D, tf), lambda i, j: (l, 0, j + nj)),
            pl.BlockSpec((None, tf, D), lambda i, j: (l, j, 0)),
        ],
        out_specs=[
            pl.BlockSpec((tm, D), lambda i, j: (i, 0)),
            pl.BlockSpec((Ms, D), lambda i, j: (0, 0)),
        ],
        out_shape=[jax.ShapeDtypeStruct((M, D), F32), jax.ShapeDtypeStruct((Ms, D), F32)],
        scratch_shapes=[pltpu.VMEM((tm + Ms, D), BF16)],
        compiler_params=_params(("arbitrary", "arbitrary")),
        name="ffn",
    )(x, xs, g, w_in, w_in, w_out)


W_TILE = 1024
SUBLANE = 8


def _wprep_kernel(w1_ref, w2_ref, o_ref, *, nA, shift):
    j = pl.program_id(1)

    @pl.when(j < nA)
    def _():
        o_ref[...] = w1_ref[...].astype(BF16)

    @pl.when(j >= nA)
    def _():
        o_ref[...] = jnp.concatenate([w1_ref[shift:, :], w2_ref[...]], axis=0).astype(BF16)


def _wprep(w_t, n_a, shift, tn):
    depth, n_in, D = w_t.shape
    assert n_a % tn == 0 and (n_in - shift) % tn == 0 and shift == SUBLANE
    nj = (n_in - shift) // tn
    kern = functools.partial(_wprep_kernel, nA=n_a // tn, shift=shift)
    return pl.pallas_call(
        kern,
        grid=(depth, nj),
        in_specs=[
            pl.BlockSpec((None, tn, D), lambda l, j: (l, j, 0)),
            pl.BlockSpec((None, shift, D), lambda l, j: (l, (j + 1) * (tn // shift), 0)),
        ],
        out_specs=pl.BlockSpec((None, tn, D), lambda l, j: (l, j, 0)),
        out_shape=jax.ShapeDtypeStruct((depth, n_in - shift, D), BF16),
        compiler_params=_params(("parallel", "parallel")),
        name="wprep",
    )(w_t, w_t)


def _norm_rows(x_ref, g_ref, h_ref, rows):
    h32 = _rms(x_ref[...], g_ref[...])
    hb = h32.astype(BF16)
    h_ref[rows, :] = hb
    return h32, hb


def _inproj_kernel(x_ref, xs_ref, g_ref, w_ref, wif_ref, o_ref, os_ref, gif_ref, gifs_ref, h_ref):
    i = pl.program_id(0)
    j = pl.program_id(1)
    tm = x_ref.shape[0]
    ts = xs_ref.shape[0]

    def gates(h32, hb):
        hl = (h32 - hb.astype(F32)).astype(BF16)
        w_hi = wif_ref[0]
        both = _dot_nt(hb, jnp.concatenate([w_hi, wif_ref[1]], axis=0))
        return both[:, 0:LANE] + both[:, LANE:2 * LANE] + _dot_nt(hl, w_hi)

    @pl.when(j == 0)
    def _():
        gif_ref[...] = gates(*_norm_rows(x_ref, g_ref, h_ref, slice(0, tm)))

    @pl.when((i == 0) & (j == 0))
    def _():
        gifs_ref[...] = gates(*_norm_rows(xs_ref, g_ref, h_ref, slice(tm, tm + ts)))

    @pl.when(i == 0)
    def _():
        res = _dot_nt(h_ref[...], w_ref[...])
        o_ref[...] = res[0:tm, :]
        os_ref[...] = res[tm:tm + ts, :]

    @pl.when(i > 0)
    def _():
        o_ref[...] = _dot_nt(h_ref[0:tm, :], w_ref[...])


def _inproj(x, xs, g, w_ab, w_if, l, tm, tn):
    M, D = x.shape
    Ms = xs.shape[0]
    nG, nA, nQ = 2 * D // tn, (QK_WIDTH + 2 * M_WIDTH) // tn, A_WIDTH // tn
    NP = 2 * D + QK_WIDTH + 2 * M_WIDTH + A_WIDTH
    nj = NP // tn
    w_idx = lambda j: jnp.where(j < nG, j + nA + 3 * nQ, j - nG)
    return pl.pallas_call(
        _inproj_kernel,
        grid=(M // tm, nj),
        in_specs=[
            pl.BlockSpec((tm, D), lambda i, j: (i, 0)),
            pl.BlockSpec((Ms, D), lambda i, j: (0, 0)),
            pl.BlockSpec((None, 1, D), lambda i, j: (l, 0, 0)),
            pl.BlockSpec((None, tn, D), lambda i, j: (l, w_idx(j), 0)),
            pl.BlockSpec((None, 2, LANE, D), lambda i, j: (l, 0, 0, 0)),
        ],
        out_specs=[
            pl.BlockSpec((tm, tn), lambda i, j: (i, j)),
            pl.BlockSpec((Ms, tn), lambda i, j: (0, jnp.where(i == 0, j, nj - 1))),
            pl.BlockSpec((tm, LANE), lambda i, j: (i, 0)),
            pl.BlockSpec((Ms, LANE), lambda i, j: (0, 0)),
        ],
        out_shape=[jax.ShapeDtypeStruct((M, NP), F32), jax.ShapeDtypeStruct((Ms, NP), F32),
                   jax.ShapeDtypeStruct((M, LANE), F32), jax.ShapeDtypeStruct((Ms, LANE), F32)],
        scratch_shapes=[pltpu.VMEM((tm + Ms, D), BF16)],
        compiler_params=_params(("arbitrary", "arbitrary")),
        name="inproj",
    )(x, xs, g, w_ab, w_if)


def _kvproj_kernel(x_ref, g_ref, w_ref, *refs, nQ):
    k_ref, v_ref, h_ref = refs[-3:]
    j = pl.program_id(1)

    @pl.when(j == 0)
    def _():
        h_ref[...] = _rms(x_ref[...], g_ref[...]).astype(BF16)

    @pl.when(j < nQ)
    def _():
        k_ref[...] = _dot_nt(h_ref[...], w_ref[...])

    @pl.when(j >= nQ)
    def _():
        v_ref[...] = _dot_nt(h_ref[...], w_ref[...])


def _kvproj(x, g, w_ab, stacks, l, tm, tn):
    M, D = x.shape
    depth = w_ab.shape[0]
    nA, nQ = (QK_WIDTH + 2 * M_WIDTH) // tn, A_WIDTH // tn
    kern = functools.partial(_kvproj_kernel, nQ=nQ)
    carried = () if stacks is None else tuple(stacks)
    return pl.pallas_call(
        kern,
        grid=(M // tm, 2 * nQ),
        in_specs=[
            pl.BlockSpec((tm, D), lambda i, j: (i, 0)),
            pl.BlockSpec((None, 1, D), lambda i, j: (l, 0, 0)),
            pl.BlockSpec((None, tn, D), lambda i, j: (l, nA + nQ + j, 0)),
        ] + [pl.BlockSpec(memory_space=pl.ANY) for _ in carried],
        out_specs=[
            pl.BlockSpec((None, tm, tn), lambda i, j: (l, i, jnp.minimum(j, nQ - 1))),
            pl.BlockSpec((None, tm, tn), lambda i, j: (l, i, jnp.maximum(j - nQ, 0))),
        ],
        out_shape=[jax.ShapeDtypeStruct((depth, M, A_WIDTH), F32)] * 2,
        scratch_shapes=[pltpu.VMEM((tm, D), BF16)],
        input_output_aliases={3 + n: n for n in range(len(carried))},
        compiler_params=_params(("parallel", "arbitrary")),
        name="kvproj",
    )(x, g, w_ab, *carried)


def _mlstm_kernel(qp_ref, kp_ref, v_ref, og_ref, gif_ref, bif_ref, cw_ref, cb_ref, mn_ref,
                  cbuf_ref, C0_ref, n0_ref, m0_ref,
                  hm_ref, C_ref, n_ref, m_ref, xq_ref, xk_ref, *, L, mm_dtype):
    c = pl.program_id(1)
    half = QK_WIDTH // 2

    @pl.when(c == 0)
    def _():
        xq_ref[0:CONV_PAD, :] = cbuf_ref[:, 0:half]
        xk_ref[0:CONV_PAD, :] = cbuf_ref[:, half:QK_WIDTH]
        C_ref[...] = C0_ref[...]
        n_ref[...] = n0_ref[...]
        m_ref[...] = m0_ref[...]

    @pl.when(c > 0)
    def _():
        xq_ref[0:CONV_PAD, :] = xq_ref[L:L + CONV_PAD, :]
        xk_ref[0:CONV_PAD, :] = xk_ref[L:L + CONV_PAD, :]

    xq_ref[CONV_PAD:CONV_PAD + L, :] = qp_ref[...]
    xk_ref[CONV_PAD:CONV_PAD + L, :] = kp_ref[...]

    cw = cw_ref[...]
    cb = cb_ref[...]

    def conv_silu(x_ref, w, b):
        xs = x_ref[...]
        y = b + xs * w[CONV_W - 1:CONV_W, :]
        for s in range(1, CONV_W):
            y = y + pltpu.roll(xs, s, axis=0) * w[CONV_W - 1 - s:CONV_W - s, :]
        y = y[CONV_PAD:, :]
        return y * jax.nn.sigmoid(y)

    qa = conv_silu(xq_ref, cw[:, 0:half], cb[:, 0:half])
    ka = conv_silu(xk_ref, cw[:, half:QK_WIDTH], cb[:, half:QK_WIDTH]) * (M_DK ** -0.5)
    g = gif_ref[...] + bif_ref[...]

    ti = lax.broadcasted_iota(jnp.int32, (L, L), 0)
    si = lax.broadcasted_iota(jnp.int32, (L, L), 1)
    eye = ti == si
    low = si <= ti
    upp = ti <= si

    for h in range(M_HEADS):
        sl = slice(h * M_DV, (h + 1) * M_DV)
        ig_col = g[:, h:h + 1]
        gf = g[:, M_HEADS + h:M_HEADS + h + 1]
        lf_col = jnp.minimum(gf, 0.0) - jnp.log1p(jnp.exp(-jnp.abs(gf)))
        ig_row = jnp.sum(jnp.where(eye, ig_col, 0.0), axis=0, keepdims=True)
        lf_row = jnp.sum(jnp.where(eye, lf_col, 0.0), axis=0, keepdims=True)
        F_col = jnp.sum(jnp.where(low, lf_row, 0.0), axis=1, keepdims=True)
        F_row = jnp.sum(jnp.where(upp, lf_col, 0.0), axis=0, keepdims=True)
        m0 = m_ref[h]
        C0 = C_ref[h]
        n0 = n_ref[h]
        logD = jnp.where(low, F_col - F_row + ig_row, -jnp.inf)
        lst = F_col + m0
        m_col = jnp.maximum(lst, jnp.max(logD, axis=1, keepdims=True))
        q = qa[:, sl]
        k = ka[:, sl]
        qb = q.astype(mm_dtype)
        kb = k.astype(mm_dtype)
        vb = v_ref[:, sl].astype(mm_dtype)
        wgt = jnp.exp(logD - m_col) * _dot_nt(qb, kb)
        sc = jnp.exp(lst - m_col)
        num = _dot(wgt.astype(mm_dtype), vb) + sc * _dot(qb, C0.astype(mm_dtype))
        den = jnp.sum(wgt, axis=1, keepdims=True) + sc * jnp.sum(q * n0, axis=1, keepdims=True)
        hh = num / jnp.maximum(jnp.abs(den), jnp.exp(-m_col))
        FT = F_col[L - 1:L, :]
        lT = FT + m0
        m_new = jnp.maximum(lT, jnp.max(FT - F_row + ig_row, axis=1, keepdims=True))
        e_col = jnp.exp(FT - F_col + ig_col - m_new)
        sT = jnp.exp(lT - m_new)
        ke = k * e_col
        C_ref[h] = sT * C0 + _dot_tn(ke.astype(mm_dtype), vb)
        n_ref[h] = sT * n0 + jnp.sum(ke, axis=0, keepdims=True)
        m_ref[h] = m_new
        hn = hh * lax.rsqrt(jnp.mean(hh * hh, axis=1, keepdims=True) + EPS)
        out = jax.nn.sigmoid(og_ref[:, sl]) * hn * mn_ref[:, sl]
        hm_ref[:, sl] = out.astype(hm_ref.dtype)


def _mlstm(proj, gif, b_if, conv_w, conv_b, m_norm, cbuf, C0, n0, m0, l, ls, B, T, col0, out_dtype, mm_dtype):
    L = CHUNK if T % CHUNK == 0 else T
    nc = T // L
    qo = col0 // M_WIDTH
    kern = functools.partial(_mlstm_kernel, L=L, mm_dtype=mm_dtype)
    row = lambda b, c: b * nc + c
    return pl.pallas_call(
        kern,
        grid=(B, nc),
        in_specs=[
            pl.BlockSpec((L, M_WIDTH), lambda b, c: (row(b, c), qo)),
            pl.BlockSpec((L, M_WIDTH), lambda b, c: (row(b, c), qo + 1)),
            pl.BlockSpec((L, M_WIDTH), lambda b, c: (row(b, c), qo + 2)),
            pl.BlockSpec((L, M_WIDTH), lambda b, c: (row(b, c), qo + 3)),
            pl.BlockSpec((L, LANE), lambda b, c: (row(b, c), 0)),
            pl.BlockSpec((None, 1, LANE), lambda b, c: (l, 0, 0)),
            pl.BlockSpec((None, CONV_W, QK_WIDTH), lambda b, c: (l, 0, 0)),
            pl.BlockSpec((None, 1, QK_WIDTH), lambda b, c: (l, 0, 0)),
            pl.BlockSpec((None, 1, M_WIDTH), lambda b, c: (l, 0, 0)),
            pl.BlockSpec((None, None, CONV_PAD, QK_WIDTH), lambda b, c: (ls, b, 0, 0)),
            pl.BlockSpec((None, None, M_HEADS, M_DK, M_DV), lambda b, c: (ls, b, 0, 0, 0)),
            pl.BlockSpec((None, None, M_HEADS, 1, M_DK), lambda b, c: (ls, b, 0, 0, 0)),
            pl.BlockSpec((None, None, M_HEADS, 1, 1), lambda b, c: (ls, b, 0, 0, 0)),
        ],
        out_specs=[
            pl.BlockSpec((L, M_WIDTH), lambda b, c: (row(b, c), 0)),
            pl.BlockSpec((None, M_HEADS, M_DK, M_DV), lambda b, c: (b, 0, 0, 0)),
            pl.BlockSpec((None, M_HEADS, 1, M_DK), lambda b, c: (b, 0, 0, 0)),
            pl.BlockSpec((None, M_HEADS, 1, 1), lambda b, c: (b, 0, 0, 0)),
        ],
        out_shape=[
            jax.ShapeDtypeStruct((B * T, M_WIDTH), out_dtype),
            jax.ShapeDtypeStruct((B, M_HEADS, M_DK, M_DV), F32),
            jax.ShapeDtypeStruct((B, M_HEADS, 1, M_DK), F32),
            jax.ShapeDtypeStruct((B, M_HEADS, 1, 1), F32),
        ],
        scratch_shapes=[pltpu.VMEM((L + CONV_PAD, M_WIDTH), F32), pltpu.VMEM((L + CONV_PAD, M_WIDTH), F32)],
        compiler_params=_params(("parallel", "arbitrary")),
        name="mlstm",
    )(proj, proj, proj, proj, gif, b_if, conv_w, conv_b, m_norm, cbuf, C0, n0, m0)


STRIDE = 4


def _attn_kernel(q_ref, k_ref, v_ref, bias_ref, o_ref,
                 p4, p16, qbs, kbs, vbs, ops, lps, o4, l4, on, ln, *, S):
    scale = A_DH ** -0.5
    QB = Q_BLOCK
    nunits = S // QB
    G = S // STRIDE
    srcs = (q_ref, k_ref, v_ref)

    def load_operands(p, get):
        qb, kb, vb = qbs.at[p], kbs.at[p], vbs.at[p]
        kb[0:QB, :] = jnp.zeros((QB, A_DH), BF16)
        vb[0:QB, 0:A_DH] = jnp.zeros((QB, A_DH), BF16)
        vb[:, A_DH:2 * A_DH] = jnp.ones((S + QB, A_DH), BF16)
        qb[...] = (get(0) * scale).astype(BF16)
        kb[QB:QB + S, :] = get(1).astype(BF16)
        vb[QB:QB + S, 0:A_DH] = get(2).astype(BF16)

    def unit(u, p, nblk, dst_o, dst_l):
        qb, kb, vb = qbs.at[p], kbs.at[p], vbs.at[p]
        r0 = u * QB
        qv = qb[r0:r0 + QB, :]
        if nblk == 1:
            kw = kb[QB + r0:2 * QB + r0, :]
            vw = vb[QB + r0:2 * QB + r0, :]
            bias = bias_ref[p, 0, :, QB:2 * QB]
        else:
            kw = kb[r0:r0 + 2 * QB, :]
            vw = vb[r0:r0 + 2 * QB, :]
            bias = bias_ref[p, 1 if u % nblk == 0 else 0]
        s = _dot_nt(qv, kw) + bias
        m = jnp.max(s, axis=1, keepdims=True)
        pv = _dot(jnp.exp(s - m).astype(BF16), vw)
        den = pv[:, A_DH:2 * A_DH]
        dst_o[r0:r0 + QB, :] = pv[:, 0:A_DH] / den
        dst_l[r0:r0 + QB, :] = m + jnp.log(den)

    load_operands(0, lambda a: srcs[a][...])
    for u in range(nunits):
        unit(u, 0, nunits, on.at[0], ln.at[0])

    for a in range(3):
        for r in range(STRIDE):
            p4[a, r * G:(r + 1) * G, :] = srcs[a][pl.ds(r, G, stride=STRIDE), :]
    load_operands(1, lambda a: p4[a])
    for u in range(nunits):
        unit(u, 1, G // QB, ops.at[0], lps.at[0])
    for r in range(STRIDE):
        on[1, pl.ds(r, G, stride=STRIDE), :] = ops[0, r * G:(r + 1) * G, :]
        ln[1, pl.ds(r, G, stride=STRIDE), :] = lps[0, r * G:(r + 1) * G, :]

    for a in range(3):
        for r in range(STRIDE):
            for c in range(STRIDE):
                u = r * STRIDE + c
                p16[a, u * QB:(u + 1) * QB, :] = p4[a, pl.ds(r * G + c, QB, stride=STRIDE), :]
    load_operands(2, lambda a: p16[a])
    for u in range(nunits):
        unit(u, 2, 1, ops.at[1], lps.at[1])
    for r in range(STRIDE):
        for c in range(STRIDE):
            u = r * STRIDE + c
            o4[pl.ds(r * G + c, QB, stride=STRIDE), :] = ops[1, u * QB:(u + 1) * QB, :]
            l4[pl.ds(r * G + c, QB, stride=STRIDE), :] = lps[1, u * QB:(u + 1) * QB, :]
    for r in range(STRIDE):
        on[2, pl.ds(r, G, stride=STRIDE), :] = o4[r * G:(r + 1) * G, :]
        ln[2, pl.ds(r, G, stride=STRIDE), :] = l4[r * G:(r + 1) * G, :]

    def combine(bi, carry):
        rows = pl.ds(pl.multiple_of(bi * QB, QB), QB)
        ls = [ln[p, rows, :] for p in range(len(PATTERNS))]
        lmax = functools.reduce(jnp.maximum, ls)
        tot = jnp.zeros((QB, A_DH), F32)
        wsum = jnp.zeros((QB, A_DH), F32)
        for p in range(len(PATTERNS)):
            a = jnp.exp(ls[p] - lmax)
            tot = tot + a * on[p, rows, :]
            wsum = wsum + a
        o_ref[rows, :] = (tot / wsum).astype(o_ref.dtype)
        return carry

    lax.fori_loop(0, nunits, combine, 0)


def _attn_prompt(proj, k_stack, v_stack, bias, l, B, S, qcol):
    assert S == N_STEPS * PATTERNS[-1][1] and PATTERNS[1][1] == STRIDE and PATTERNS[2][1] == STRIDE * STRIDE
    H = A_HEADS
    cb = qcol // A_DH
    kern = functools.partial(_attn_kernel, S=S)
    npat = len(PATTERNS)
    return pl.pallas_call(
        kern,
        grid=(B, H),
        in_specs=[
            pl.BlockSpec((S, A_DH), lambda b, h: (b, cb + h)),
            pl.BlockSpec((None, S, A_DH), lambda b, h: (l, b, h)),
            pl.BlockSpec((None, S, A_DH), lambda b, h: (l, b, h)),
            pl.BlockSpec((None, npat, 2, Q_BLOCK, 2 * Q_BLOCK), lambda b, h: (h, 0, 0, 0, 0)),
        ],
        out_specs=pl.BlockSpec((S, A_DH), lambda b, h: (b, h)),
        out_shape=jax.ShapeDtypeStruct((B * S, A_WIDTH), BF16),
        scratch_shapes=[
            pltpu.VMEM((3, S, A_DH), F32),
            pltpu.VMEM((3, S, A_DH), F32),
            pltpu.VMEM((npat, S, A_DH), BF16),
            pltpu.VMEM((npat, S + Q_BLOCK, A_DH), BF16),
            pltpu.VMEM((npat, S + Q_BLOCK, 2 * A_DH), BF16),
            pltpu.VMEM((2, S, A_DH), F32),
            pltpu.VMEM((2, S, A_DH), F32),
            pltpu.VMEM((S, A_DH), F32),
            pltpu.VMEM((S, A_DH), F32),
            pltpu.VMEM((npat, S, A_DH), F32),
            pltpu.VMEM((npat, S, A_DH), F32),
        ],
        compiler_params=_params(("parallel", "parallel")),
        name="attn_prompt",
    )(proj, k_stack, v_stack, bias)


def _attn_s_kernel(q_ref, kn_ref, vn_ref, kc_ref, vc_ref, bc_ref, bn_ref, o_ref, *, Lbuf):
    H = A_HEADS
    npat = len(PATTERNS)
    for h in range(H):
        hs = slice(h * A_DH, (h + 1) * A_DH)
        q = q_ref[:, hs] * (A_DH ** -0.5)
        kc = kc_ref[pl.ds(h, Lbuf, stride=H), :]
        vc = vc_ref[pl.ds(h, Lbuf, stride=H), :]
        s_c = _dot_nt(q, kc)
        s_n = _dot_nt(q, kn_ref[:, hs])
        mx = None
        for p in range(npat):
            mp = jnp.maximum(jnp.max(s_c + bc_ref[h, p], axis=1, keepdims=True),
                             jnp.max(s_n + bn_ref[h, p], axis=1, keepdims=True))
            mx = mp if mx is None else jnp.maximum(mx, mp)
        e_c = jnp.exp(s_c + bc_ref[h, 0] - mx)
        e_n = jnp.exp(s_n + bn_ref[h, 0] - mx)
        for p in range(1, npat):
            e_c = e_c + jnp.exp(s_c + bc_ref[h, p] - mx)
            e_n = e_n + jnp.exp(s_n + bn_ref[h, p] - mx)
        num = _dot(e_c, vc) + _dot(e_n, vn_ref[:, hs])
        den = jnp.sum(e_c, axis=1, keepdims=True) + jnp.sum(e_n, axis=1, keepdims=True)
        o_ref[:, hs] = num / den


def _attn_sample(proj, k_stack, v_stack, cache_k, cache_v, bias_c, bias_n, l, B, T, qcol):
    H = A_HEADS
    Lbuf = cache_k.shape[2] // H
    npat = len(PATTERNS)
    kern = functools.partial(_attn_s_kernel, Lbuf=Lbuf)
    return pl.pallas_call(
        kern,
        grid=(B,),
        in_specs=[
            pl.BlockSpec((T, A_WIDTH), lambda b: (b, qcol // A_WIDTH)),
            pl.BlockSpec((None, T, A_WIDTH), lambda b: (l, b, 0)),
            pl.BlockSpec((None, T, A_WIDTH), lambda b: (l, b, 0)),
            pl.BlockSpec((None, None, Lbuf * H, A_DH), lambda b: (l, b, 0, 0)),
            pl.BlockSpec((None, None, Lbuf * H, A_DH), lambda b: (l, b, 0, 0)),
            pl.BlockSpec((H, npat, T, Lbuf), lambda b: (0, 0, 0, 0)),
            pl.BlockSpec((H, npat, T, T), lambda b: (0, 0, 0, 0)),
        ],
        out_specs=pl.BlockSpec((T, A_WIDTH), lambda b: (b, 0)),
        out_shape=jax.ShapeDtypeStruct((B * T, A_WIDTH), F32),
        compiler_params=_params(("parallel",)),
        name="attn_sample",
    )(proj, k_stack, v_stack, cache_k, cache_v, bias_c, bias_n)


def _outproj_kernel(x_ref, hm_ref, ha_ref, gm_ref, ga_ref, wpm_ref, wpa_ref, wo_ref, o_ref):
    pm = _dot(hm_ref[...].astype(BF16), wpm_ref[...])
    pa = _dot(ha_ref[...].astype(BF16), wpa_ref[...])
    merged = jax.nn.sigmoid(gm_ref[...]) * pm + jax.nn.sigmoid(ga_ref[...]) * pa
    o_ref[...] = x_ref[...] + _dot(merged.astype(BF16), wo_ref[...])


def _outproj(x, hm, ha, proj, w_pm, w_pa, w_out, l, tm):
    M, D = x.shape
    once = pl.Buffered(1)
    return pl.pallas_call(
        _outproj_kernel,
        grid=(M // tm,),
        in_specs=[
            pl.BlockSpec((tm, D), lambda i: (i, 0)),
            pl.BlockSpec((tm, M_WIDTH), lambda i: (i, 0)),
            pl.BlockSpec((tm, A_WIDTH), lambda i: (i, 0)),
            pl.BlockSpec((tm, D), lambda i: (i, 0)),
            pl.BlockSpec((tm, D), lambda i: (i, 1)),
            pl.BlockSpec((None, M_WIDTH, D), lambda i: (l, 0, 0), pipeline_mode=once),
            pl.BlockSpec((None, A_WIDTH, D), lambda i: (l, 0, 0), pipeline_mode=once),
            pl.BlockSpec((None, D, D), lambda i: (l, 0, 0), pipeline_mode=once),
        ],
        out_specs=pl.BlockSpec((tm, D), lambda i: (i, 0)),
        out_shape=jax.ShapeDtypeStruct((M, D), F32),
        compiler_params=_params(("parallel",)),
        name="outproj",
    )(x, hm, ha, proj, proj, w_pm, w_pa, w_out)


def _norm_kernel(x_ref, g_ref, o_ref):
    o_ref[...] = _rms(x_ref[...], g_ref[...])


def _final_norm(x, g, tm):
    M, D = x.shape
    return pl.pallas_call(
        _norm_kernel,
        grid=(M // tm,),
        in_specs=[pl.BlockSpec((tm, D), lambda i: (i, 0)), pl.BlockSpec((1, D), lambda i: (0, 0))],
        out_specs=pl.BlockSpec((tm, D), lambda i: (i, 0)),
        out_shape=jax.ShapeDtypeStruct((M, D), F32),
        compiler_params=_params(("parallel",)),
        name="final_norm",
    )(x, g)


def _t5_bucket(dist):
    exact = N_BUCKETS // 2
    d32 = jnp.maximum(dist, 1).astype(F32)
    large = exact + (jnp.log(d32 / exact) / math.log(MAX_DISTANCE / exact) * (N_BUCKETS - exact)).astype(jnp.int32)
    large = jnp.minimum(large, N_BUCKETS - 1)
    return jnp.where(dist < exact, dist, large)


def _step_biases(rel_table):
    assert all(w // d == N_STEPS for w, d in PATTERNS)
    buckets = jnp.stack([_t5_bucket(d * jnp.arange(N_STEPS + 1, dtype=jnp.int32)) for (w, d) in PATTERNS])
    onehot = (buckets[:, :, None] == jnp.arange(N_BUCKETS)[None, None, :]).astype(F32)
    return jnp.einsum("pjn,nh->phj", onehot, rel_table.astype(F32), precision=lax.Precision.HIGHEST)


def _toeplitz(g, rows, cols, off):
    n = g.shape[-1]
    width = off + cols
    assert width <= n - 1 and off - (rows - 1) >= 0
    flat = jnp.tile(g, (1,) * (g.ndim - 1) + (rows,))[..., :rows * (n - 1)]
    return flat.reshape(g.shape[:-1] + (rows, n - 1))[..., off:off + cols]


def _prompt_bias(sb):
    Q = Q_BLOCK
    ninf = lambda n: jnp.full(sb.shape[:-1] + (n,), -jnp.inf, F32)
    g = jnp.concatenate([ninf(Q - 1), sb[..., ::-1], ninf(Q)], axis=-1)
    base = _toeplitz(g, Q, 2 * Q, Q - 1)
    first = jnp.where(jnp.arange(2 * Q) >= Q, base, -jnp.inf)
    return jnp.transpose(jnp.stack([base, first], axis=2), (1, 0, 2, 3, 4))


def _sample_bias(sb, T, Lbuf):
    n = Lbuf + T
    tabs = []
    for p, (w, d) in enumerate(PATTERNS):
        bd = jnp.concatenate([sb[p][..., None], jnp.full(sb[p].shape + (d - 1,), -jnp.inf, F32)], axis=-1)
        bd = bd.reshape(sb.shape[1], -1)
        bd = jnp.concatenate([bd, jnp.full((sb.shape[1], n), -jnp.inf, F32)], axis=-1)[:, :n]
        g = jnp.concatenate([bd[:, ::-1], jnp.full((sb.shape[1], T), -jnp.inf, F32)], axis=-1)
        tabs.append(_toeplitz(g, T, n, T - 1))
    tab = jnp.stack(tabs, axis=1)
    return tab[..., :Lbuf], tab[..., Lbuf:]


def _row_tile(M, cap):
    return cap if M % cap == 0 else M


class _Path:
    def __init__(self, x3, wts, sb, states):
        self.wts = wts
        self.shape = x3.shape
        self.prompt = states is None
        self.depth = wts["w_ab"].shape[0]
        self._setup(x3, sb, states)

    def _setup(self, x3, sb, states):
        B, T, D = self.shape
        wts, depth, prompt = self.wts, self.depth, self.prompt
        M = B * T
        self.tm = _row_tile(M, 1024)
        if prompt:
            self.bias = _prompt_bias(sb)
            self.cbuf = jnp.zeros((1, B, CONV_PAD, QK_WIDTH), F32)
            self.C0 = jnp.zeros((1, B, M_HEADS, M_DK, M_DV), F32)
            self.n0 = jnp.zeros((1, B, M_HEADS, 1, M_DK), F32)
            self.m0 = jnp.zeros((1, B, M_HEADS, 1, 1), F32)
        else:
            cache_k, cache_v, sC, sn, sm, sconv = states
            Lbuf = cache_k.shape[2]
            self.bias_c, self.bias_n = _sample_bias(sb, T, Lbuf)
            self.cache_k = cache_k.astype(F32).reshape(depth, B, Lbuf * A_HEADS, A_DH)
            self.cache_v = cache_v.astype(F32).reshape(depth, B, Lbuf * A_HEADS, A_DH)
            self.cbuf = jnp.pad(sconv.astype(F32), ((0, 0), (0, 0), (CONV_PAD - (CONV_W - 1), 0), (0, 0)))
            self.C0 = sC.astype(F32)
            self.n0 = sn.astype(F32).reshape(depth, B, M_HEADS, 1, M_DK)
            self.m0 = sm.astype(F32).reshape(depth, B, M_HEADS, 1, 1)
        self.x0 = x3.reshape(M, D)
        self.stacks = (jnp.zeros((depth, M, A_WIDTH), F32), jnp.zeros((depth, M, A_WIDTH), F32))
        self.Cs, self.ns, self.ms, self.convs = [], [], [], []

    def mixer(self, x, proj, gif, l):
        B, T, D = self.shape
        wts, prompt, tm = self.wts, self.prompt, self.tm
        col_m = 2 * D
        col_q = 2 * D + QK_WIDTH + 2 * M_WIDTH
        self.stacks = _kvproj(x, wts["ln_mix"], wts["w_ab"], self.stacks, l, tm, W_TILE)
        k_stack, v_stack = self.stacks
        ls = 0 if prompt else l
        hm, C1, n1, m1 = _mlstm(proj, gif, wts["b_if"], wts["conv_w"], wts["conv_b"], wts["m_norm"],
                                self.cbuf, self.C0, self.n0, self.m0, l, ls, B, T, col_m,
                                BF16 if prompt else F32, BF16 if prompt else F32)
        if prompt:
            ha = _attn_prompt(proj, k_stack, v_stack, self.bias, l, B, T, col_q)
        else:
            ha = _attn_sample(proj, k_stack, v_stack, self.cache_k, self.cache_v, self.bias_c, self.bias_n,
                              l, B, T, col_q)
        self.Cs.append(C1)
        self.ns.append(n1.reshape(B, M_HEADS, M_DK))
        self.ms.append(m1.reshape(B, M_HEADS))
        pre = proj.reshape(B, T, -1)[:, :, col_m:col_m + QK_WIDTH]
        if T >= CONV_W - 1:
            self.convs.append(pre[:, T - (CONV_W - 1):])
        else:
            self.convs.append(jnp.concatenate([self.cbuf[ls][:, CONV_PAD - (CONV_W - 1) + T:], pre], axis=1))
        return _outproj(x, hm, ha, proj, wts["w_pm"], wts["w_pa"], wts["w_out"], l, _row_tile(B * T, 512))

    def finish(self, x):
        B, T, D = self.shape
        y = _final_norm(x, self.wts["ln_f"], self.tm).reshape(B, T, D)
        stk = lambda xs: jnp.stack(xs, axis=0)
        keep = min(MAX_WINDOW, T)
        k_stack, v_stack = self.stacks
        k_new = k_stack.reshape(self.depth, B, T, A_HEADS, A_DH)[:, :, T - keep:]
        v_new = v_stack.reshape(self.depth, B, T, A_HEADS, A_DH)[:, :, T - keep:]
        return y, (k_new, v_new, stk(self.Cs), stk(self.ns), stk(self.ms), stk(self.convs))


def kernel(x_prompt, x_sample, cache_k, cache_v, state_C, state_n, state_m, state_conv, w_in, conv_w, conv_b, b_if, m_norm, w_pm, w_pa, w_out, rel_table, ln_ffa, w_ffa_in, w_ffa_out, ln_mix, ln_ffb, w_ffb_in, w_ffb_out, ln_f):
    depth, D, _ = w_in.shape
    o_if = QK_WIDTH + 2 * M_WIDTH
    o_a = o_if + 2 * M_HEADS
    w_t = jnp.swapaxes(w_in, 1, 2)
    w_if32 = jnp.pad(w_t[:, o_if:o_a, :], ((0, 0), (0, LANE - 2 * M_HEADS), (0, 0)))
    w_if_hi = w_if32.astype(BF16)
    w_if_lo = (w_if32 - w_if_hi.astype(F32)).astype(BF16)
    wts = dict(
        w_ab=_wprep(w_t, o_if, o_a - o_if, W_TILE),
        w_if=jnp.stack([w_if_hi, w_if_lo], axis=1),
        b_if=jnp.pad(b_if.astype(F32), ((0, 0), (0, LANE - 2 * M_HEADS))).reshape(depth, 1, LANE),
        conv_w=conv_w.astype(F32),
        conv_b=conv_b.astype(F32).reshape(depth, 1, QK_WIDTH),
        m_norm=m_norm.astype(F32).reshape(depth, 1, M_WIDTH),
        w_pm=w_pm.astype(BF16), w_pa=w_pa.astype(BF16), w_out=w_out.astype(BF16),
        ln_ffa=ln_ffa.astype(F32).reshape(depth, 1, D), ln_mix=ln_mix.astype(F32).reshape(depth, 1, D),
        ln_ffb=ln_ffb.astype(F32).reshape(depth, 1, D), ln_f=ln_f.astype(F32).reshape(1, D),
        w_ffa_in=w_ffa_in.astype(BF16), w_ffa_out=w_ffa_out.astype(BF16),
        w_ffb_in=w_ffb_in.astype(BF16), w_ffb_out=w_ffb_out.astype(BF16),
    )
    sb = _step_biases(rel_table)
    prompt = _Path(x_prompt, wts, sb, None)
    sample = _Path(x_sample, wts, sb, (cache_k, cache_v, state_C, state_n, state_m, state_conv))
    FF = wts["w_ffa_out"].shape[1]
    tf = 512 if FF % 512 == 0 else FF
    xp, xs = prompt.x0, sample.x0
    for l in range(depth):
        xp, xs = _ffn(xp, xs, wts["ln_ffa"], wts["w_ffa_in"], wts["w_ffa_out"], l, prompt.tm, tf)
        proj_p, proj_s, gif_p, gif_s = _inproj(xp, xs, wts["ln_mix"], wts["w_ab"], wts["w_if"], l, prompt.tm, W_TILE)
        xp = prompt.mixer(xp, proj_p, gif_p, l)
        xs = sample.mixer(xs, proj_s, gif_s, l)
        xp, xs = _ffn(xp, xs, wts["ln_ffb"], wts["w_ffb_in"], wts["w_ffb_out"], l, prompt.tm, tf)
    y_p, (k_p, v_p, C_p, n_p, m_p, conv_p) = prompt.finish(xp)
    y_s, (k_s, v_s, C_s, n_s, m_s, conv_s) = sample.finish(xs)
    return (y_p, y_s, k_p, v_p, C_p, n_p, m_p, conv_p, k_s, v_s, C_s, n_s, m_s, conv_s)
```
